```python
import functools
import jax, jax.numpy as jnp
from jax import lax
import numpy as np

D_MODEL = 1024
BATCH = 16
SEQ = 4096
DEPTH = 1
DEC_BATCH = 16
DEC_SEQ = 64
PAST_LEN = 1024

CHUNK = 64
HEAD_DIM = 64
A_Q_HEADS = 8
A_KV_HEADS = 2
A_GROUP = A_Q_HEADS // A_KV_HEADS
A_WINDOW = 128
A_BAND_CHUNKS = A_WINDOW // CHUNK + 1
A_REACH = (A_BAND_CHUNKS - 1) * CHUNK
B_HEADS = 4
B_PREV_CHUNKS = 8
B_BAND_CHUNKS = B_PREV_CHUNKS + 1
B_REACH = B_PREV_CHUNKS * CHUNK
REL_CLIP = 128
C_HEADS = 4
N_MEM = 256
FF_DIM = 2816
ROPE_THETA = 10000.0
EPS = 1e-6
NEG = -1e30
N_BRANCH = 3

A_Q = A_Q_HEADS * HEAD_DIM
A_KV = A_KV_HEADS * HEAD_DIM
B_W = B_HEADS * HEAD_DIM
C_W = C_HEADS * HEAD_DIM
IN_COLS = A_Q + 2 * A_KV + 3 * B_W + C_W
SPLITS = (A_Q, A_Q + A_KV, A_Q + 2 * A_KV, A_Q + 2 * A_KV + B_W,
          A_Q + 2 * A_KV + 2 * B_W, A_Q + 2 * A_KV + 3 * B_W)

kernel_name = 'hybrid_streaming_encoder_step'


def rmsnorm(x, g):
    xf = x.astype(jnp.float32)
    y = xf * lax.rsqrt(jnp.mean(xf * xf, axis=-1, keepdims=True) + EPS)
    return (y * g.astype(jnp.float32)).astype(x.dtype)


def swiglu(x, w_gate, w_up, w_down):
    return (jax.nn.silu(x @ w_gate) * (x @ w_up)) @ w_down


def rope(x, pos):
    half = HEAD_DIM // 2
    inv = ROPE_THETA ** (-jnp.arange(half, dtype=jnp.float32) / half)
    ang = pos.astype(jnp.float32)[:, None] * inv[None, :]
    cos = jnp.cos(ang)[None, :, None, :]
    sin = jnp.sin(ang)[None, :, None, :]
    xf = x.astype(jnp.float32)
    x1, x2 = xf[..., :half], xf[..., half:]
    return jnp.concatenate([x1 * cos - x2 * sin, x1 * sin + x2 * cos], axis=-1).astype(x.dtype)


def band_mask(q_pos, k_pos, band_chunks):
    qc = q_pos[:, None] // CHUNK
    kc = k_pos[None, :] // CHUNK
    return (k_pos[None, :] >= 0) & (kc <= qc) & (qc - kc < band_chunks)


def swa_sink_attend(q, k, v, q_pos, k_pos, sinks):
    b, sq = q.shape[:2]
    qg = q.reshape(b, sq, A_KV_HEADS, A_GROUP, HEAD_DIM)
    s = jnp.einsum('bqkgd,bskd->bkgqs', qg, k).astype(jnp.float32) * (HEAD_DIM ** -0.5)
    s = jnp.where(band_mask(q_pos, k_pos, A_BAND_CHUNKS), s, NEG)
    sink = sinks.astype(jnp.float32).reshape(1, A_KV_HEADS, A_GROUP, 1, 1)
    sink = jnp.broadcast_to(sink, s.shape[:-1] + (1,))
    p = jax.nn.softmax(jnp.concatenate([s, sink], axis=-1), axis=-1)[..., :-1]
    o = jnp.einsum('bkgqs,bskd->bqkgd', p.astype(v.dtype), v)
    return o.reshape(b, sq, A_Q)


def chunk_relpos_attend(q, k, v, q_pos, k_pos, rel_bias):
    b, sq = q.shape[:2]
    s = jnp.einsum('bqhd,bshd->bhqs', q, k).astype(jnp.float32) * (HEAD_DIM ** -0.5)
    rel = jnp.clip(q_pos[:, None] - k_pos[None, :], -REL_CLIP, REL_CLIP) + REL_CLIP
    s = s + rel_bias.astype(jnp.float32)[:, rel][None]
    s = jnp.where(band_mask(q_pos, k_pos, B_BAND_CHUNKS), s, NEG)
    p = jax.nn.softmax(s, axis=-1)
    o = jnp.einsum('bhqs,bshd->bqhd', p.astype(v.dtype), v)
    return o.reshape(b, sq, B_W)


def mem_attend(q, mk, mv):
    b, sq = q.shape[:2]
    s = jnp.einsum('bqhd,bmhd->bhqm', q, mk).astype(jnp.float32) * (HEAD_DIM ** -0.5)
    p = jax.nn.softmax(s, axis=-1)
    o = jnp.einsum('bhqm,bmhd->bqhd', p.astype(mv.dtype), mv)
    return o.reshape(b, sq, C_W)


def memory_kv(mem, g_mem, w_mem_kv, g_kc):
    m = rmsnorm(mem, g_mem) @ w_mem_kv
    b, n = m.shape[:2]
    mk = rmsnorm(m[..., :C_W].reshape(b, n, C_HEADS, HEAD_DIM), g_kc)
    mv = m[..., C_W:].reshape(b, n, C_HEADS, HEAD_DIM)
    return mk, mv


def sweep_chunks(core, reach, q, k, v):
    b, s = q.shape[:2]
    nc = s // CHUNK
    kp = jnp.pad(k, ((0, 0), (reach, 0), (0, 0), (0, 0)))
    vp = jnp.pad(v, ((0, 0), (reach, 0), (0, 0), (0, 0)))

    def one(c):
        start = c * CHUNK
        qc = lax.dynamic_slice_in_dim(q, start, CHUNK, axis=1)
        kc = lax.dynamic_slice_in_dim(kp, start, reach + CHUNK, axis=1)
        vc = lax.dynamic_slice_in_dim(vp, start, reach + CHUNK, axis=1)
        q_pos = start + jnp.arange(CHUNK, dtype=jnp.int32)
        k_pos = start - reach + jnp.arange(reach + CHUNK, dtype=jnp.int32)
        return core(qc, kc, vc, q_pos, k_pos)

    out = lax.map(one, jnp.arange(nc, dtype=jnp.int32))
    return out.transpose(1, 0, 2, 3).reshape(b, s, out.shape[-1])


def cached_attend(core, cache_k, cache_v, q, k, v):
    n_past = cache_k.shape[1]
    k_all = jnp.concatenate([cache_k, k], axis=1)
    v_all = jnp.concatenate([cache_v, v], axis=1)
    k_pos = jnp.arange(PAST_LEN - n_past, PAST_LEN + q.shape[1], dtype=jnp.int32)
    return core(q, k_all, v_all, k_pos[n_past:], k_pos)


def trunk_layer(x, pos, mem_k, mem_v, attn_a, attn_b, w):
    b, s, _ = x.shape
    x = x + 0.5 * swiglu(rmsnorm(x, w['g_ff1']), w['w_ff1_gate'], w['w_ff1_up'], w['w_ff1_down'])
    h = rmsnorm(x, w['g_mix'])
    qa, ka, va, qb, kb, vb, qc = jnp.split(h @ w['w_in'], SPLITS, axis=-1)
    heads = lambda t, n: t.reshape(b, s, n, HEAD_DIM)
    qa = rope(rmsnorm(heads(qa, A_Q_HEADS), w['g_qa']), pos)
    ka = rope(rmsnorm(heads(ka, A_KV_HEADS), w['g_ka']), pos)
    va = heads(va, A_KV_HEADS)
    qb = rmsnorm(heads(qb, B_HEADS), w['g_qb'])
    kb = rmsnorm(heads(kb, B_HEADS), w['g_kb'])
    vb = heads(vb, B_HEADS)
    qc = rmsnorm(heads(qc, C_HEADS), w['g_qc'])
    ya = attn_a(qa, ka, va)
    yb = attn_b(qb, kb, vb)
    yc = mem_attend(qc, mem_k, mem_v)
    gates = jax.nn.sigmoid((h @ w['w_gate'] + w['b_gate']).astype(jnp.float32))
    gates = gates.astype(x.dtype).reshape(b, s, N_BRANCH, D_MODEL)
    merged = (gates[..., 0, :] * (ya @ w['w_br_a']) + gates[..., 1, :] * (yb @ w['w_br_b'])
              + gates[..., 2, :] * (yc @ w['w_br_c']))
    x = x + merged @ w['w_out']
    x = x + 0.5 * swiglu(rmsnorm(x, w['g_ff2']), w['w_ff2_gate'], w['w_ff2_up'], w['w_ff2_down'])
    return rmsnorm(x, w['g_final']), (ka, va, kb, vb)


def setup_inputs(seed: int = 0) -> dict:
    key = jax.random.key(seed)
    ks = iter(jax.random.split(key, 48))
    nrm = lambda shape, scale: jax.random.normal(next(ks), shape, jnp.float32) * scale
    gain = lambda shape: 1.0 + nrm(shape, 0.01)
    L = DEPTH
    na = min(A_REACH, PAST_LEN)
    nb = min(B_REACH, PAST_LEN)
    return {
        'x_prompt': nrm((BATCH, SEQ, D_MODEL), 1.0),
        'x_sample': nrm((DEC_BATCH, DEC_SEQ, D_MODEL), 1.0),
        'cache_a_k': nrm((L, DEC_BATCH, na, A_KV_HEADS, HEAD_DIM), 1.0),
        'cache_a_v': nrm((L, DEC_BATCH, na, A_KV_HEADS, HEAD_DIM), 1.0),
        'cache_b_k': nrm((L, DEC_BATCH, nb, B_HEADS, HEAD_DIM), 1.0),
        'cache_b_v': nrm((L, DEC_BATCH, nb, B_HEADS, HEAD_DIM), 1.0),
        'cache_mem_k': nrm((L, DEC_BATCH, N_MEM, C_HEADS, HEAD_DIM), 1.0),
        'cache_mem_v': nrm((L, DEC_BATCH, N_MEM, C_HEADS, HEAD_DIM), 1.0),
        'mem_prompt': nrm((BATCH, N_MEM, D_MODEL), 1.0),
        'g_ff1': gain((L, D_MODEL)),
        'w_ff1_gate': nrm((L, D_MODEL, FF_DIM), D_MODEL ** -0.5),
        'w_ff1_up': nrm((L, D_MODEL, FF_DIM), D_MODEL ** -0.5),
        'w_ff1_down': nrm((L, FF_DIM, D_MODEL), FF_DIM ** -0.5),
        'g_mix': gain((L, D_MODEL)),
        'w_in': nrm((L, D_MODEL, IN_COLS), D_MODEL ** -0.5),
        'g_qa': gain((L, HEAD_DIM)),
        'g_ka': gain((L, HEAD_DIM)),
        'sinks_a': nrm((L, A_Q_HEADS), 0.5),
        'g_qb': gain((L, HEAD_DIM)),
        'g_kb': gain((L, HEAD_DIM)),
        'rel_bias_b': nrm((L, B_HEADS, 2 * REL_CLIP + 1), 0.1),
        'g_qc': gain((L, HEAD_DIM)),
        'g_mem': gain((L, D_MODEL)),
        'w_mem_kv': nrm((L, D_MODEL, 2 * C_W), D_MODEL ** -0.5),
        'g_kc': gain((L, HEAD_DIM)),
        'w_gate': nrm((L, D_MODEL, N_BRANCH * D_MODEL), D_MODEL ** -0.5),
        'b_gate': nrm((L, N_BRANCH * D_MODEL), 0.01),
        'w_br_a': nrm((L, A_Q, D_MODEL), A_Q ** -0.5),
        'w_br_b': nrm((L, B_W, D_MODEL), B_W ** -0.5),
        'w_br_c': nrm((L, C_W, D_MODEL), C_W ** -0.5),
        'w_out': nrm((L, D_MODEL, D_MODEL), D_MODEL ** -0.5),
        'g_ff2': gain((L, D_MODEL)),
        'w_ff2_gate': nrm((L, D_MODEL, FF_DIM), D_MODEL ** -0.5),
        'w_ff2_up': nrm((L, D_MODEL, FF_DIM), D_MODEL ** -0.5),
        'w_ff2_down': nrm((L, FF_DIM, D_MODEL), FF_DIM ** -0.5),
        'g_final': gain((L, D_MODEL)),
    }


def reference(x_prompt, x_sample, cache_a_k, cache_a_v, cache_b_k, cache_b_v, cache_mem_k, cache_mem_v,
              mem_prompt, g_ff1, w_ff1_gate, w_ff1_up, w_ff1_down, g_mix, w_in, g_qa, g_ka, sinks_a,
              g_qb, g_kb, rel_bias_b, g_qc, g_mem, w_mem_kv, g_kc, w_gate, b_gate, w_br_a, w_br_b,
              w_br_c, w_out, g_ff2, w_ff2_gate, w_ff2_up, w_ff2_down, g_final):
    s_p = x_prompt.shape[1]
    s_s = x_sample.shape[1]
    pos_p = jnp.arange(s_p, dtype=jnp.int32)
    pos_s = PAST_LEN + jnp.arange(s_s, dtype=jnp.int32)
    keep_a = min(A_REACH, s_p)
    keep_b = min(B_REACH, s_p)
    y_p, y_s = x_prompt, x_sample
    akp, avp, bkp, bvp, mkp, mvp, aks, avs, bks, bvs = ([] for _ in range(10))
    for l in range(DEPTH):
        w = dict(g_ff1=g_ff1[l], w_ff1_gate=w_ff1_gate[l], w_ff1_up=w_ff1_up[l], w_ff1_down=w_ff1_down[l],
                 g_mix=g_mix[l], w_in=w_in[l], g_qa=g_qa[l], g_ka=g_ka[l], g_qb=g_qb[l], g_kb=g_kb[l],
                 g_qc=g_qc[l], w_gate=w_gate[l], b_gate=b_gate[l], w_br_a=w_br_a[l], w_br_b=w_br_b[l],
                 w_br_c=w_br_c[l], w_out=w_out[l], g_ff2=g_ff2[l], w_ff2_gate=w_ff2_gate[l],
                 w_ff2_up=w_ff2_up[l], w_ff2_down=w_ff2_down[l], g_final=g_final[l])
        core_a = functools.partial(swa_sink_attend, sinks=sinks_a[l])
        core_b = functools.partial(chunk_relpos_attend, rel_bias=rel_bias_b[l])
        mk, mv = memory_kv(mem_prompt, g_mem[l], w_mem_kv[l], g_kc[l])
        y_p, (ka, va, kb, vb) = trunk_layer(
            y_p, pos_p, mk, mv,
            functools.partial(sweep_chunks, core_a, A_REACH),
            functools.partial(sweep_chunks, core_b, B_REACH), w)
        akp.append(ka[:, s_p - keep_a:]); avp.append(va[:, s_p - keep_a:])
        bkp.append(kb[:, s_p - keep_b:]); bvp.append(vb[:, s_p - keep_b:])
        mkp.append(mk); mvp.append(mv)
        y_s, (ka, va, kb, vb) = trunk_layer(
            y_s, pos_s, cache_mem_k[l], cache_mem_v[l],
            functools.partial(cached_attend, core_a, cache_a_k[l], cache_a_v[l]),
            functools.partial(cached_attend, core_b, cache_b_k[l], cache_b_v[l]), w)
        aks.append(ka); avs.append(va); bks.append(kb); bvs.append(vb)
    return (y_p, y_s, jnp.stack(akp), jnp.stack(avp), jnp.stack(bkp), jnp.stack(bvp),
            jnp.stack(mkp), jnp.stack(mvp), jnp.stack(aks), jnp.stack(avs), jnp.stack(bks), jnp.stack(bvs))
```

```python
import functools

import jax
import jax.numpy as jnp
from jax import lax
from jax.experimental import pallas as pl
from jax.experimental.pallas import tpu as pltpu

D_MODEL = 1024
PAST_LEN = 1024
CHUNK = 64
HEAD_DIM = 64
A_Q_HEADS = 8
A_KV_HEADS = 2
A_GROUP = A_Q_HEADS // A_KV_HEADS
A_PREV_CHUNKS = 2
B_HEADS = 4
B_PREV_CHUNKS = 8
REL_CLIP = 128
C_HEADS = 4
N_MEM = 256
FF_DIM = 2816
ROPE_THETA = 10000.0
EPS = 1e-6
NEG = -1e30

A_Q = A_Q_HEADS * HEAD_DIM
A_KV = A_KV_HEADS * HEAD_DIM
B_W = B_HEADS * HEAD_DIM
C_W = C_HEADS * HEAD_DIM
IN_COLS = A_Q + 2 * A_KV + 3 * B_W + C_W
A_KEYS = (A_PREV_CHUNKS + 1) * CHUNK
B_KEYS = (B_PREV_CHUNKS + 1) * CHUNK
OFF_QA, OFF_KA, OFF_VA = 0, A_Q, A_Q + A_KV
OFF_QB, OFF_KB, OFF_VB = A_Q + 2 * A_KV, A_Q + 2 * A_KV + B_W, A_Q + 2 * A_KV + 2 * B_W
OFF_QC = A_Q + 2 * A_KV + 3 * B_W
KV_COLS = 2 * A_KV + 2 * B_W
A_HEAD_ORDER = (0, 4, 1, 5, 2, 6, 3, 7)

V7X_LANES = 128
V7X_MXU_DIM = 256
V7X_VMEM_LIMIT = 56 * 1024 * 1024

TOKEN_TILE = 512
FF_SPLIT = 2
BF16 = jnp.bfloat16
F32 = jnp.float32


def _dot(a, b):
    return jnp.dot(a, b, preferred_element_type=F32)


def _dot_nt(a, b):
    return lax.dot_general(a, b, (((1,), (1,)), ((), ())), preferred_element_type=F32)


def _rmsnorm(x, g):
    return x * lax.rsqrt(jnp.mean(x * x, axis=-1, keepdims=True) + EPS) * g


def _sigmoid(z):
    return 1.0 / (1.0 + jnp.exp(-z))


def _const_spec(shape):
    nd = len(shape)
    return pl.BlockSpec(shape, lambda *_: (0,) * nd, pipeline_mode=pl.Buffered(1))


def _params(*sem):
    return pltpu.CompilerParams(dimension_semantics=sem, vmem_limit_bytes=V7X_VMEM_LIMIT)


def _swiglu_residual(x, g_pre, wg_ref, wu_ref, wd_ref):
    hn = _rmsnorm(x, g_pre).astype(BF16)
    fc = FF_DIM // FF_SPLIT
    acc = None
    for c in range(FF_SPLIT):
        g = _dot(hn, wg_ref[:, c * fc:(c + 1) * fc])
        u = _dot(hn, wu_ref[:, c * fc:(c + 1) * fc])
        a = (g * _sigmoid(g) * u).astype(BF16)
        d = _dot(a, wd_ref[c * fc:(c + 1) * fc, :])
        acc = d if acc is None else acc + d
    return x + 0.5 * acc


def _ffn_kernel(x_ref, gpre_ref, wg_ref, wu_ref, wd_ref, gpost_ref, *out_refs, emit_x):
    y = _swiglu_residual(x_ref[...], gpre_ref[...], wg_ref, wu_ref, wd_ref)
    if emit_x:
        out_refs[0][...] = y
    n_ref = out_refs[-1]
    n_ref[...] = _rmsnorm(y, gpost_ref[...]).astype(n_ref.dtype)


def _ffn(x, g_pre, wg, wu, wd, g_post, *, emit_x, norm_dtype):
    n = x.shape[0]
    tile = pl.BlockSpec((TOKEN_TILE, D_MODEL), lambda i: (i, 0))
    out_shape = [jax.ShapeDtypeStruct((n, D_MODEL), norm_dtype)]
    out_specs = [tile]
    if emit_x:
        out_shape.insert(0, jax.ShapeDtypeStruct((n, D_MODEL), F32))
        out_specs.insert(0, tile)
    return pl.pallas_call(
        functools.partial(_ffn_kernel, emit_x=emit_x),
        grid=(n // TOKEN_TILE,),
        in_specs=[tile, _const_spec((1, D_MODEL)), _const_spec(wg.shape), _const_spec(wu.shape),
                  _const_spec(wd.shape), _const_spec((1, D_MODEL))],
        out_specs=out_specs,
        out_shape=out_shape,
        compiler_params=_params("parallel"),
        name="ffn_x" if emit_x else "ffn_final",
    )(x, g_pre, wg, wu, wd, g_post)


def _head_inv_rms(y, bd):
    w = y.shape[1]
    ssq = _dot((y * y).astype(BF16), bd[:w, :w])
    return lax.rsqrt(ssq * (1.0 / HEAD_DIM) + EPS)


def _rope(x, cos, sin_signed, first_half):
    rot = jnp.where(first_half, pltpu.roll(x, V7X_LANES - HEAD_DIM // 2, 1), pltpu.roll(x, HEAD_DIM // 2, 1))
    return x * cos + rot * sin_signed


def _proj_kernel(h_ref, w_ref, gain_ref, cos_ref, sin_ref, bd_ref, qkv_ref, kv32_ref):
    t = pl.program_id(1)
    last = pl.num_programs(1) - 1
    y = _dot(h_ref[0], w_ref[...])
    bd = bd_ref[...]
    cos = cos_ref[...]
    sin = sin_ref[...]
    lane = lax.broadcasted_iota(jnp.int32, (TOKEN_TILE, V7X_LANES), 1)
    first_half = (lane % HEAD_DIM) < (HEAD_DIM // 2)

    def normed(off, width):
        blk = y[:, off:off + width]
        return blk * _head_inv_rms(blk, bd) * gain_ref[:, off:off + width]

    pieces = {}
    for off in range(OFF_QA, OFF_QA + A_Q, V7X_MXU_DIM):
        n = normed(off, V7X_MXU_DIM)
        for s in range(0, V7X_MXU_DIM, V7X_LANES):
            pieces[off + s] = _rope(n[:, s:s + V7X_LANES], cos, sin, first_half)
    pieces[OFF_KA] = _rope(normed(OFF_KA, A_KV), cos, sin, first_half)
    pieces[OFF_VA] = y[:, OFF_VA:OFF_VA + A_KV]
    for off in (OFF_QB, OFF_KB, OFF_QC):
        n = normed(off, V7X_MXU_DIM)
        for s in range(0, V7X_MXU_DIM, V7X_LANES):
            pieces[off + s] = n[:, s:s + V7X_LANES]
    for s in range(0, B_W, V7X_LANES):
        pieces[OFF_VB + s] = y[:, OFF_VB + s:OFF_VB + s + V7X_LANES]

    for off, val in pieces.items():
        qkv_ref[0, :, off:off + V7X_LANES] = val.astype(BF16)

    @pl.when(t == last)
    def _():
        col = 0
        for off, width in ((OFF_KA, A_KV), (OFF_VA, A_KV), (OFF_KB, B_W), (OFF_VB, B_W)):
            for s in range(0, width, V7X_LANES):
                kv32_ref[0, :, col:col + V7X_LANES] = pieces[off + s]
                col += V7X_LANES


def _proj(h, w_in, gain, cos, sin, bd):
    b, s, _ = h.shape
    return pl.pallas_call(
        _proj_kernel,
        grid=(b, s // TOKEN_TILE),
        in_specs=[pl.BlockSpec((1, TOKEN_TILE, D_MODEL), lambda i, t: (i, t, 0)),
                  _const_spec(w_in.shape), _const_spec(gain.shape),
                  pl.BlockSpec((TOKEN_TILE, V7X_LANES), lambda i, t: (t, 0)),
                  pl.BlockSpec((TOKEN_TILE, V7X_LANES), lambda i, t: (t, 0)),
                  _const_spec(bd.shape)],
        out_specs=[pl.BlockSpec((1, TOKEN_TILE, IN_COLS), lambda i, t: (i, t, 0)),
                   pl.BlockSpec((1, TOKEN_TILE, KV_COLS), lambda i, t: (i, 0, 0))],
        out_shape=[jax.ShapeDtypeStruct((b, s, IN_COLS), BF16),
                   jax.ShapeDtypeStruct((b, TOKEN_TILE, KV_COLS), F32)],
        compiler_params=_params("parallel", "arbitrary"),
        name="proj",
    )(h, w_in, gain, cos, sin, bd)


def _memkv_kernel(m_ref, gmem_ref, w_ref, gkc_ref, bd_ref, o32_ref, o16_ref):
    hn = _rmsnorm(m_ref[...], gmem_ref[...]).astype(BF16)
    y = _dot(hn, w_ref[...])
    k = y[:, :C_W]
    k = k * _head_inv_rms(k, bd_ref[...]) * gkc_ref[...]
    v = y[:, C_W:]
    o32_ref[:, :C_W] = k
    o32_ref[:, C_W:] = v
    o16_ref[:, :C_W] = k.astype(BF16)
    o16_ref[:, C_W:] = v.astype(BF16)


def _memkv(mem, g_mem, w, g_kc, bd):
    n = mem.shape[0]
    return pl.pallas_call(
        _memkv_kernel,
        grid=(n // TOKEN_TILE,),
        in_specs=[pl.BlockSpec((TOKEN_TILE, D_MODEL), lambda i: (i, 0)), _const_spec((1, D_MODEL)),
                  _const_spec(w.shape), _const_spec((1, C_W)), _const_spec(bd.shape)],
        out_specs=[pl.BlockSpec((TOKEN_TILE, 2 * C_W), lambda i: (i, 0))] * 2,
        out_shape=[jax.ShapeDtypeStruct((n, 2 * C_W), F32), jax.ShapeDtypeStruct((n, 2 * C_W), BF16)],
        compiler_params=_params("parallel"),
        name="memkv",
    )(mem, g_mem, w, g_kc, bd)


def _softmax_pv(s, v, sink=None):
    m = jnp.max(s, axis=-1, keepdims=True)
    if sink is not None:
        m = jnp.maximum(m, sink)
    e = jnp.exp(s - m)
    l = jnp.sum(e, axis=-1, keepdims=True)
    if sink is not None:
        l = l + jnp.exp(sink - m)
    return _dot(e.astype(BF16), v) * (1.0 / l)


def _attn_kernel(qa_ref, qb_ref, qc_ref, ka_ref, va_ref, kb_ref, vb_ref, mk_ref, mv_ref,
                 bias_ref, maska_ref, sink_ref, sela_ref, selb_ref, y_ref, *, n_chunks, off_a, off_b):
    t = pl.program_id(1)
    sink = sink_ref[:, 0:1]
    lane_a = lax.broadcasted_iota(jnp.int32, (CHUNK, V7X_LANES), 1)
    low_half = lane_a < HEAD_DIM
    lane_b = lax.broadcasted_iota(jnp.int32, (CHUNK, B_W), 1) // HEAD_DIM

    def stack_heads(q, sel_ref, n_heads):
        return jnp.concatenate([q * sel_ref[h] for h in range(n_heads)], axis=0)

    def unstack_heads(r, n_heads):
        out = jnp.where(lane_b == 0, r[0:CHUNK], 0.0)
        for h in range(1, n_heads):
            out = out + jnp.where(lane_b == h, r[h * CHUNK:(h + 1) * CHUNK], 0.0)
        return out

    def chunk_body(j, carry):
        c = t * n_chunks + j
        rows = pl.ds(pl.multiple_of(j * CHUNK, CHUNK), CHUNK)

        ca = c + off_a
        start_a = pl.multiple_of(jnp.maximum(ca - A_PREV_CHUNKS, 0) * CHUNK, CHUNK)
        var_a = jnp.minimum(ca, A_PREV_CHUNKS)
        qa = qa_ref[0, rows, :]
        lhs = jnp.concatenate(
            [qa[:, (p // 2) * V7X_LANES:(p // 2 + 1) * V7X_LANES] * sela_ref[p % 2] for p in range(A_Q_HEADS)],
            axis=0)
        s = _dot_nt(lhs, ka_ref[0, pl.ds(start_a, A_KEYS), :]) + maska_ref[var_a][0:1]
        r = _softmax_pv(s, va_ref[0, pl.ds(start_a, A_KEYS), :], sink)
        for jc in range(A_Q // V7X_LANES):
            ev = r[(2 * jc) * CHUNK:(2 * jc + 1) * CHUNK]
            od = r[(2 * jc + 1) * CHUNK:(2 * jc + 2) * CHUNK]
            y_ref[0, rows, jc * V7X_LANES:(jc + 1) * V7X_LANES] = jnp.where(low_half, ev, od).astype(BF16)

        cb = c + off_b
        start_b = pl.multiple_of(jnp.maximum(cb - B_PREV_CHUNKS, 0) * CHUNK, CHUNK)
        var_b = jnp.minimum(cb, B_PREV_CHUNKS)
        lhs = stack_heads(qb_ref[0, rows, :], selb_ref, B_HEADS)
        s = _dot_nt(lhs, kb_ref[0, pl.ds(start_b, B_KEYS), :]) + bias_ref[var_b]
        r = _softmax_pv(s, vb_ref[0, pl.ds(start_b, B_KEYS), :])
        y_ref[0, rows, A_Q:A_Q + B_W] = unstack_heads(r, B_HEADS).astype(BF16)

        lhs = stack_heads(qc_ref[0, rows, :], selb_ref, C_HEADS)
        r = _softmax_pv(_dot_nt(lhs, mk_ref[0]), mv_ref[0])
        y_ref[0, rows, A_Q + B_W:] = unstack_heads(r, C_HEADS).astype(BF16)
        return carry

    lax.fori_loop(0, n_chunks, chunk_body, 0)


def _attn(q_arrays, kv_arrays, mk, mv, bias, mask_a, sink, sel_a, sel_b, *, sq, n_chunks, off_a, off_b):
    b = q_arrays[0][0].shape[0]
    tq = n_chunks * CHUNK
    q_widths = (A_Q, B_W, C_W)
    kv_widths = (A_KV, A_KV, B_W, B_W)
    in_specs, args = [], []
    for (arr, cb), w in zip(q_arrays, q_widths):
        in_specs.append(pl.BlockSpec((1, tq, w), lambda i, t, cb=cb: (i, t, cb)))
        args.append(arr)
    for (arr, cb), w in zip(kv_arrays, kv_widths):
        in_specs.append(pl.BlockSpec((1, arr.shape[1], w), lambda i, t, cb=cb: (i, 0, cb)))
        args.append(arr)
    for arr in (mk, mv):
        in_specs.append(pl.BlockSpec((1, N_MEM, C_W), lambda i, t: (i, 0, 0)))
        args.append(arr)
    for arr in (bias, mask_a, sink, sel_a, sel_b):
        in_specs.append(_const_spec(arr.shape))
        args.append(arr)
    return pl.pallas_call(
        functools.partial(_attn_kernel, n_chunks=n_chunks, off_a=off_a, off_b=off_b),
        grid=(b, sq // tq),
        in_specs=in_specs,
        out_specs=pl.BlockSpec((1, tq, D_MODEL), lambda i, t: (i, t, 0)),
        out_shape=jax.ShapeDtypeStruct((b, sq, D_MODEL), BF16),
        compiler_params=_params("parallel", "arbitrary"),
        name="attn",
    )(*args)


def _merge_kernel(h_ref, y_ref, x_ref, wgate_ref, bgate_ref, wa_ref, wb_ref, wc_ref, wout_ref, o_ref):
    h = h_ref[...]
    merged = None
    col = 0
    for i, (w_ref, width) in enumerate(((wa_ref, A_Q), (wb_ref, B_W), (wc_ref, C_W))):
        gate = _sigmoid(_dot(h, wgate_ref[:, i * D_MODEL:(i + 1) * D_MODEL])
                        + bgate_ref[:, i * D_MODEL:(i + 1) * D_MODEL])
        term = gate * _dot(y_ref[:, col:col + width], w_ref[...])
        merged = term if merged is None else merged + term
        col += width
    o_ref[...] = x_ref[...] + _dot(merged.astype(BF16), wout_ref[...])


def _merge(h, y, x, w_gate, b_gate, wa, wb, wc, w_out):
    n = h.shape[0]
    tile = pl.BlockSpec((TOKEN_TILE, D_MODEL), lambda i: (i, 0))
    consts = (w_gate, b_gate, wa, wb, wc, w_out)
    return pl.pallas_call(
        _merge_kernel,
        grid=(n // TOKEN_TILE,),
        in_specs=[tile, tile, tile] + [_const_spec(c.shape) for c in consts],
        out_specs=tile,
        out_shape=jax.ShapeDtypeStruct((n, D_MODEL), F32),
        compiler_params=_params("parallel"),
        name="merge",
    )(h, y, x, *consts)


def _rope_tables(pos):
    half = HEAD_DIM // 2
    inv = ROPE_THETA ** (-jnp.arange(half, dtype=F32) / half)
    ang = pos.astype(F32)[:, None] * inv[None, :]
    cos, sin = jnp.cos(ang), jnp.sin(ang)
    reps = V7X_LANES // HEAD_DIM
    return jnp.tile(jnp.concatenate([cos, cos], -1), (1, reps)), jnp.tile(jnp.concatenate([-sin, sin], -1), (1, reps))


def _bias_tables(rel_bias):
    i = jnp.arange(CHUNK)[:, None]
    j = jnp.arange(B_KEYS)[None, :]
    tables = []
    for v in range(B_PREV_CHUNKS + 1):
        q_pos = v * CHUNK + i
        k_pos = j if v < B_PREV_CHUNKS else j
        rel = jnp.clip(q_pos - k_pos, -REL_CLIP, REL_CLIP) + REL_CLIP
        tab = rel_bias.astype(F32)[:, rel]
        visible = (k_pos // CHUNK) <= v
        tables.append(jnp.where(visible[None], tab, NEG).reshape(B_HEADS * CHUNK, B_KEYS))
    return jnp.stack(tables)


def _mask_a_tables():
    j = jnp.arange(A_KEYS)[None, :]
    rows = [jnp.where((j // CHUNK) <= v, 0.0, NEG) for v in range(A_PREV_CHUNKS + 1)]
    return jnp.stack([jnp.broadcast_to(r, (8, A_KEYS)) for r in rows]).astype(F32)


def _block_diag_ones():
    i = jnp.arange(V7X_MXU_DIM)
    return (i[:, None] // HEAD_DIM == i[None, :] // HEAD_DIM).astype(BF16)


def _lane_selectors():
    lane = jnp.arange(V7X_LANES)[None, :] // HEAD_DIM
    sel_a = jnp.stack([jnp.broadcast_to(lane == p, (CHUNK, V7X_LANES)) for p in range(2)]).astype(BF16)
    lane = jnp.arange(B_W)[None, :] // HEAD_DIM
    sel_b = jnp.stack([jnp.broadcast_to(lane == h, (CHUNK, B_W)) for h in range(B_HEADS)]).astype(BF16)
    return sel_a, sel_b


def kernel(x_prompt, x_sample, cache_a_k, cache_a_v, cache_b_k, cache_b_v, cache_mem_k, cache_mem_v,
           mem_prompt, g_ff1, w_ff1_gate, w_ff1_up, w_ff1_down, g_mix, w_in, g_qa, g_ka, sinks_a,
           g_qb, g_kb, rel_bias_b, g_qc, g_mem, w_mem_kv, g_kc, w_gate, b_gate, w_br_a, w_br_b,
           w_br_c, w_out, g_ff2, w_ff2_gate, w_ff2_up, w_ff2_down, g_final):
    bp, sp, _ = x_prompt.shape
    bs, ss, _ = x_sample.shape
    l = 0
    row = lambda g: g[l].reshape(1, -1).astype(F32)
    bf = lambda w: w[l].astype(BF16)

    order = jnp.array(A_HEAD_ORDER)
    head_cols = (order[:, None] * HEAD_DIM + jnp.arange(HEAD_DIM)[None, :]).reshape(-1)
    w_in_l = w_in[l]
    w_in_bf = jnp.concatenate([w_in_l[:, :A_Q][:, head_cols], w_in_l[:, A_Q:]], axis=1).astype(BF16)
    w_br_a_bf = w_br_a[l][head_cols].astype(BF16)
    scale = HEAD_DIM ** -0.5
    ones = lambda n: jnp.ones((n,), F32)
    gain = jnp.concatenate([
        jnp.tile(g_qa[l], A_Q_HEADS) * scale, jnp.tile(g_ka[l], A_KV_HEADS), ones(A_KV),
        jnp.tile(g_qb[l], B_HEADS) * scale, jnp.tile(g_kb[l], B_HEADS), ones(B_W),
        jnp.tile(g_qc[l], C_HEADS) * scale]).reshape(1, IN_COLS).astype(F32)
    sink = jnp.broadcast_to(jnp.repeat(sinks_a[l][order].astype(F32), CHUNK)[:, None], (A_Q_HEADS * CHUNK, V7X_LANES))
    bias = _bias_tables(rel_bias_b[l])
    mask_a = _mask_a_tables()
    bd = _block_diag_ones()
    sel_a, sel_b = _lane_selectors()
    ff1 = (row(g_ff1), bf(w_ff1_gate), bf(w_ff1_up), bf(w_ff1_down), row(g_mix))
    ff2 = (row(g_ff2), bf(w_ff2_gate), bf(w_ff2_up), bf(w_ff2_down), row(g_final))
    merge_w = (bf(w_gate), row(b_gate), w_br_a_bf, bf(w_br_b), bf(w_br_c), bf(w_out))

    def trunk(x, pos, batch_view, attend):
        n = x.shape[0] * x.shape[1]
        xf = x.reshape(n, D_MODEL)
        x1, h = _ffn(xf, *ff1, emit_x=True, norm_dtype=BF16)
        cos, sin = _rope_tables(pos)
        qkv, kv32 = _proj(h.reshape(batch_view + (D_MODEL,)), w_in_bf, gain, cos, sin, bd)
        y = attend(qkv)
        x2 = _merge(h, y.reshape(n, D_MODEL), x1, *merge_w)
        (out,) = _ffn(x2, *ff2, emit_x=False, norm_dtype=F32)
        return out.reshape(x.shape), kv32

    def split_kv(kv32, b, rows):
        kv32 = kv32.reshape(b, rows, KV_COLS)
        ka = kv32[..., :A_KV].reshape(b, rows, A_KV_HEADS, HEAD_DIM)
        va = kv32[..., A_KV:2 * A_KV].reshape(b, rows, A_KV_HEADS, HEAD_DIM)
        kb = kv32[..., 2 * A_KV:2 * A_KV + B_W].reshape(b, rows, B_HEADS, HEAD_DIM)
        vb = kv32[..., 2 * A_KV + B_W:].reshape(b, rows, B_HEADS, HEAD_DIM)
        return ka, va, kb, vb

    g_kc_row = jnp.tile(g_kc[l], C_HEADS).reshape(1, C_W).astype(F32)
    mem32, mem16 = _memkv(mem_prompt.reshape(bp * N_MEM, D_MODEL), row(g_mem), bf(w_mem_kv), g_kc_row, bd)
    mem16 = mem16.reshape(bp, N_MEM, 2 * C_W)
    mk_p = mem32[:, :C_W].reshape(1, bp, N_MEM, C_HEADS, HEAD_DIM)
    mv_p = mem32[:, C_W:].reshape(1, bp, N_MEM, C_HEADS, HEAD_DIM)

    def attend_prompt(qkv):
        q_arrays = ((qkv, OFF_QA // A_Q), (qkv, OFF_QB // B_W), (qkv, OFF_QC // C_W))
        kv_arrays = ((qkv, OFF_KA // A_KV), (qkv, OFF_VA // A_KV), (qkv, OFF_KB // B_W), (qkv, OFF_VB // B_W))
        return _attn(q_arrays, kv_arrays, mem16[..., :C_W], mem16[..., C_W:], bias, mask_a, sink, sel_a, sel_b,
                     sq=sp, n_chunks=TOKEN_TILE // CHUNK, off_a=0, off_b=0)

    y_p, kv32_p = trunk(x_prompt, jnp.arange(sp, dtype=jnp.int32), (bp, sp), attend_prompt)
    ka, va, kb, vb = split_kv(kv32_p, bp, TOKEN_TILE)
    keep_a = min(A_PREV_CHUNKS * CHUNK, sp)
    keep_b = min(B_PREV_CHUNKS * CHUNK, sp)
    prompt_caches = (ka[:, TOKEN_TILE - keep_a:][None], va[:, TOKEN_TILE - keep_a:][None],
                     kb[:, TOKEN_TILE - keep_b:][None], vb[:, TOKEN_TILE - keep_b:][None])

    n_s = bs * ss
    rows_per_tile = TOKEN_TILE // ss
    pos_s = PAST_LEN + jnp.tile(jnp.arange(ss, dtype=jnp.int32), rows_per_tile)
    flat16 = lambda c: c[l].reshape(c.shape[1], c.shape[2], -1).astype(BF16)
    ca_k, ca_v, cb_k, cb_v = flat16(cache_a_k), flat16(cache_a_v), flat16(cache_b_k), flat16(cache_b_v)
    cm_k, cm_v = flat16(cache_mem_k), flat16(cache_mem_v)

    def attend_sample(qkv):
        qkv = qkv.reshape(bs, ss, IN_COLS)
        new = lambda off, w: qkv[..., off:off + w]
        kv_arrays = ((jnp.concatenate([ca_k, new(OFF_KA, A_KV)], 1), 0), (jnp.concatenate([ca_v, new(OFF_VA, A_KV)], 1), 0),
                     (jnp.concatenate([cb_k, new(OFF_KB, B_W)], 1), 0), (jnp.concatenate([cb_v, new(OFF_VB, B_W)], 1), 0))
        q_arrays = ((qkv, OFF_QA // A_Q), (qkv, OFF_QB // B_W), (qkv, OFF_QC // C_W))
        return _attn(q_arrays, kv_arrays, cm_k, cm_v, bias, mask_a, sink, sel_a, sel_b,
                     sq=ss, n_chunks=1, off_a=ca_k.shape[1] // CHUNK, off_b=cb_k.shape[1] // CHUNK)

    y_s, kv32_s = trunk(x_sample, pos_s, (n_s // TOKEN_TILE, TOKEN_TILE), attend_sample)
    ka_s, va_s, kb_s, vb_s = split_kv(kv32_s, bs, ss)

    return (y_p, y_s, *prompt_caches, mk_p, mv_p, ka_s[None], va_s[None], kb_s[None], vb_s[None])
```

```python
import functools

import jax
import jax.numpy as jnp
from jax import lax
from jax.experimental import pallas as pl
from jax.experimental.pallas import tpu as pltpu

D_MODEL = 1024
PAST_LEN = 1024
CHUNK = 64
HEAD_DIM = 64
A_Q_HEADS = 8
A_KV_HEADS = 2
A_GROUP = A_Q_HEADS // A_KV_HEADS
A_PREV_CHUNKS = 2
B_HEADS = 4
B_PREV_CHUNKS = 8
REL_CLIP = 128
C_HEADS = 4
N_MEM = 256
FF_DIM = 2816
ROPE_THETA = 10000.0
EPS = 1e-6
NEG = -1e30

A_Q = A_Q_HEADS * HEAD_DIM
A_KV = A_KV_HEADS * HEAD_DIM
B_W = B_HEADS * HEAD_DIM
C_W = C_HEADS * HEAD_DIM
IN_COLS = A_Q + 2 * A_KV + 3 * B_W + C_W
A_KEYS = (A_PREV_CHUNKS + 1) * CHUNK
B_KEYS = (B_PREV_CHUNKS + 1) * CHUNK
OFF_QA, OFF_KA, OFF_VA = 0, A_Q, A_Q + A_KV
OFF_QB, OFF_KB, OFF_VB = A_Q + 2 * A_KV, A_Q + 2 * A_KV + B_W, A_Q + 2 * A_KV + 2 * B_W
OFF_QC = A_Q + 2 * A_KV + 3 * B_W
KV_COLS = 2 * A_KV + 2 * B_W
A_HEAD_ORDER = (0, 4, 1, 5, 2, 6, 3, 7)

V7X_LANES = 128
V7X_MXU_DIM = 256
V7X_VMEM_LIMIT = 56 * 1024 * 1024

TOKEN_TILE = 512
FF_SPLIT = 2
CHUNK_UNROLL = 2
BF16 = jnp.bfloat16
F32 = jnp.float32


def _dot(a, b):
    return jnp.dot(a, b, preferred_element_type=F32)


def _dot_nt(a, b):
    return lax.dot_general(a, b, (((1,), (1,)), ((), ())), preferred_element_type=F32)


def _rmsnorm(x, g):
    return x * lax.rsqrt(jnp.mean(x * x, axis=-1, keepdims=True) + EPS) * g


def _sigmoid(z):
    return 1.0 / (1.0 + jnp.exp(-z))


def _const_spec(shape):
    nd = len(shape)
    return pl.BlockSpec(shape, lambda *_: (0,) * nd, pipeline_mode=pl.Buffered(1))


def _params(*sem):
    return pltpu.CompilerParams(dimension_semantics=sem, vmem_limit_bytes=V7X_VMEM_LIMIT)


def _swiglu_residual(x, g_pre, wg_ref, wu_ref, wd_ref):
    hn = _rmsnorm(x, g_pre).astype(BF16)
    fc = FF_DIM // FF_SPLIT
    acc = None
    for c in range(FF_SPLIT):
        g = _dot(hn, wg_ref[:, c * fc:(c + 1) * fc])
        u = _dot(hn, wu_ref[:, c * fc:(c + 1) * fc])
        a = (g * _sigmoid(g) * u).astype(BF16)
        d = _dot(a, wd_ref[c * fc:(c + 1) * fc, :])
        acc = d if acc is None else acc + d
    return x + 0.5 * acc


def _ffn_kernel(x_ref, gpre_ref, wg_ref, wu_ref, wd_ref, gpost_ref, *out_refs, emit_x):
    y = _swiglu_residual(x_ref[...], gpre_ref[...], wg_ref, wu_ref, wd_ref)
    if emit_x:
        out_refs[0][...] = y
    n_ref = out_refs[-1]
    n_ref[...] = _rmsnorm(y, gpost_ref[...]).astype(n_ref.dtype)


def _ffn(x, g_pre, wg, wu, wd, g_post, *, emit_x, norm_dtype):
    n = x.shape[0]
    tile = pl.BlockSpec((TOKEN_TILE, D_MODEL), lambda i: (i, 0))
    out_shape = [jax.ShapeDtypeStruct((n, D_MODEL), norm_dtype)]
    out_specs = [tile]
    if emit_x:
        out_shape.insert(0, jax.ShapeDtypeStruct((n, D_MODEL), F32))
        out_specs.insert(0, tile)
    return pl.pallas_call(
        functools.partial(_ffn_kernel, emit_x=emit_x),
        grid=(n // TOKEN_TILE,),
        in_specs=[tile, _const_spec((1, D_MODEL)), _const_spec(wg.shape), _const_spec(wu.shape),
                  _const_spec(wd.shape), _const_spec((1, D_MODEL))],
        out_specs=out_specs,
        out_shape=out_shape,
        compiler_params=_params("parallel"),
        name="ffn_x" if emit_x else "ffn_final",
    )(x, g_pre, wg, wu, wd, g_post)


def _head_inv_rms(y, bd):
    w = y.shape[1]
    ssq = _dot((y * y).astype(BF16), bd[:w, :w])
    return lax.rsqrt(ssq * (1.0 / HEAD_DIM) + EPS)


def _rope(x, cos, sin_signed, first_half):
    rot = jnp.where(first_half, pltpu.roll(x, V7X_LANES - HEAD_DIM // 2, 1), pltpu.roll(x, HEAD_DIM // 2, 1))
    return x * cos + rot * sin_signed


def _proj_kernel(h_ref, w_ref, gain_ref, cos_ref, sin_ref, bd_ref, qkv_ref, kv32_ref):
    t = pl.program_id(1)
    last = pl.num_programs(1) - 1
    y = _dot(h_ref[0], w_ref[...])
    bd = bd_ref[...]
    cos = cos_ref[...]
    sin = sin_ref[...]
    lane = lax.broadcasted_iota(jnp.int32, (TOKEN_TILE, V7X_LANES), 1)
    first_half = (lane % HEAD_DIM) < (HEAD_DIM // 2)

    def normed(off, width):
        blk = y[:, off:off + width]
        return blk * _head_inv_rms(blk, bd) * gain_ref[:, off:off + width]

    pieces = {}
    for off in range(OFF_QA, OFF_QA + A_Q, V7X_MXU_DIM):
        n = normed(off, V7X_MXU_DIM)
        for s in range(0, V7X_MXU_DIM, V7X_LANES):
            pieces[off + s] = _rope(n[:, s:s + V7X_LANES], cos, sin, first_half)
    pieces[OFF_KA] = _rope(normed(OFF_KA, A_KV), cos, sin, first_half)
    pieces[OFF_VA] = y[:, OFF_VA:OFF_VA + A_KV]
    for off in (OFF_QB, OFF_KB, OFF_QC):
        n = normed(off, V7X_MXU_DIM)
        for s in range(0, V7X_MXU_DIM, V7X_LANES):
            pieces[off + s] = n[:, s:s + V7X_LANES]
    for s in range(0, B_W, V7X_LANES):
        pieces[OFF_VB + s] = y[:, OFF_VB + s:OFF_VB + s + V7X_LANES]

    for off, val in pieces.items():
        qkv_ref[0, :, off:off + V7X_LANES] = val.astype(BF16)

    @pl.when(t == last)
    def _():
        col = 0
        for off, width in ((OFF_KA, A_KV), (OFF_VA, A_KV), (OFF_KB, B_W), (OFF_VB, B_W)):
            for s in range(0, width, V7X_LANES):
                kv32_ref[0, :, col:col + V7X_LANES] = pieces[off + s]
                col += V7X_LANES


def _proj(h, w_in, gain, cos, sin, bd):
    b, s, _ = h.shape
    return pl.pallas_call(
        _proj_kernel,
        grid=(b, s // TOKEN_TILE),
        in_specs=[pl.BlockSpec((1, TOKEN_TILE, D_MODEL), lambda i, t: (i, t, 0)),
                  _const_spec(w_in.shape), _const_spec(gain.shape),
                  pl.BlockSpec((TOKEN_TILE, V7X_LANES), lambda i, t: (t, 0)),
                  pl.BlockSpec((TOKEN_TILE, V7X_LANES), lambda i, t: (t, 0)),
                  _const_spec(bd.shape)],
        out_specs=[pl.BlockSpec((1, TOKEN_TILE, IN_COLS), lambda i, t: (i, t, 0)),
                   pl.BlockSpec((1, TOKEN_TILE, KV_COLS), lambda i, t: (i, 0, 0))],
        out_shape=[jax.ShapeDtypeStruct((b, s, IN_COLS), BF16),
                   jax.ShapeDtypeStruct((b, TOKEN_TILE, KV_COLS), F32)],
        compiler_params=_params("parallel", "arbitrary"),
        name="proj",
    )(h, w_in, gain, cos, sin, bd)


def _memkv_kernel(m_ref, gmem_ref, w_ref, gkc_ref, bd_ref, o32_ref, o16_ref):
    hn = _rmsnorm(m_ref[...], gmem_ref[...]).astype(BF16)
    y = _dot(hn, w_ref[...])
    k = y[:, :C_W]
    k = k * _head_inv_rms(k, bd_ref[...]) * gkc_ref[...]
    v = y[:, C_W:]
    o32_ref[:, :C_W] = k
    o32_ref[:, C_W:] = v
    o16_ref[:, :C_W] = k.astype(BF16)
    o16_ref[:, C_W:] = v.astype(BF16)


def _memkv(mem, g_mem, w, g_kc, bd):
    n = mem.shape[0]
    return pl.pallas_call(
        _memkv_kernel,
        grid=(n // TOKEN_TILE,),
        in_specs=[pl.BlockSpec((TOKEN_TILE, D_MODEL), lambda i: (i, 0)), _const_spec((1, D_MODEL)),
                  _const_spec(w.shape), _const_spec((1, C_W)), _const_spec(bd.shape)],
        out_specs=[pl.BlockSpec((TOKEN_TILE, 2 * C_W), lambda i: (i, 0))] * 2,
        out_shape=[jax.ShapeDtypeStruct((n, 2 * C_W), F32), jax.ShapeDtypeStruct((n, 2 * C_W), BF16)],
        compiler_params=_params("parallel"),
        name="memkv",
    )(mem, g_mem, w, g_kc, bd)


def _softmax_pv(s, v, sink=None):
    m = jnp.max(s, axis=-1, keepdims=True)
    if sink is not None:
        m = jnp.maximum(m, sink)
    e = jnp.exp(s - m)
    l = jnp.sum(e, axis=-1, keepdims=True)
    if sink is not None:
        l = l + jnp.exp(sink - m)
    return _dot(e.astype(BF16), v) * (1.0 / l)


def _attn_kernel(qa_ref, qb_ref, qc_ref, ka_ref, va_ref, kb_ref, vb_ref, mk_ref, mv_ref,
                 bias_ref, maska_ref, sink_ref, sela_ref, selb_ref, y_ref, *, n_chunks, off_a, off_b):
    t = pl.program_id(1)
    sink = sink_ref[:, 0:1]
    lane_a = lax.broadcasted_iota(jnp.int32, (CHUNK, V7X_LANES), 1)
    low_half = lane_a < HEAD_DIM
    lane_b = lax.broadcasted_iota(jnp.int32, (CHUNK, B_W), 1) // HEAD_DIM

    def stack_heads(q, sel_ref, n_heads):
        return jnp.concatenate([q * sel_ref[h] for h in range(n_heads)], axis=0)

    def unstack_heads(r, n_heads):
        out = jnp.where(lane_b == 0, r[0:CHUNK], 0.0)
        for h in range(1, n_heads):
            out = out + jnp.where(lane_b == h, r[h * CHUNK:(h + 1) * CHUNK], 0.0)
        return out

    def chunk_body(j, carry):
        c = t * n_chunks + j
        rows = pl.ds(pl.multiple_of(j * CHUNK, CHUNK), CHUNK)

        ca = c + off_a
        start_a = pl.multiple_of(jnp.maximum(ca - A_PREV_CHUNKS, 0) * CHUNK, CHUNK)
        var_a = jnp.minimum(ca, A_PREV_CHUNKS)
        qa = qa_ref[0, rows, :]
        lhs = jnp.concatenate(
            [qa[:, (p // 2) * V7X_LANES:(p // 2 + 1) * V7X_LANES] * sela_ref[p % 2] for p in range(A_Q_HEADS)],
            axis=0)
        s = _dot_nt(lhs, ka_ref[0, pl.ds(start_a, A_KEYS), :]) + maska_ref[var_a][0:1]
        r = _softmax_pv(s, va_ref[0, pl.ds(start_a, A_KEYS), :], sink)
        for jc in range(A_Q // V7X_LANES):
            ev = r[(2 * jc) * CHUNK:(2 * jc + 1) * CHUNK]
            od = r[(2 * jc + 1) * CHUNK:(2 * jc + 2) * CHUNK]
            y_ref[0, rows, jc * V7X_LANES:(jc + 1) * V7X_LANES] = jnp.where(low_half, ev, od).astype(BF16)

        cb = c + off_b
        start_b = pl.multiple_of(jnp.maximum(cb - B_PREV_CHUNKS, 0) * CHUNK, CHUNK)
        var_b = jnp.minimum(cb, B_PREV_CHUNKS)
        lhs = stack_heads(qb_ref[0, rows, :], selb_ref, B_HEADS)
        s = _dot_nt(lhs, kb_ref[0, pl.ds(start_b, B_KEYS), :]) + bias_ref[var_b]
        r = _softmax_pv(s, vb_ref[0, pl.ds(start_b, B_KEYS), :])
        y_ref[0, rows, A_Q:A_Q + B_W] = unstack_heads(r, B_HEADS).astype(BF16)

        lhs = stack_heads(qc_ref[0, rows, :], selb_ref, C_HEADS)
        r = _softmax_pv(_dot_nt(lhs, mk_ref[0]), mv_ref[0])
        y_ref[0, rows, A_Q + B_W:] = unstack_heads(r, C_HEADS).astype(BF16)
        return carry

    lax.fori_loop(0, n_chunks, chunk_body, 0, unroll=min(n_chunks, CHUNK_UNROLL))


def _attn(q_arrays, kv_arrays, mk, mv, bias, mask_a, sink, sel_a, sel_b, *, sq, n_chunks, off_a, off_b):
    b = q_arrays[0][0].shape[0]
    tq = n_chunks * CHUNK
    q_widths = (A_Q, B_W, C_W)
    kv_widths = (A_KV, A_KV, B_W, B_W)
    in_specs, args = [], []
    for (arr, cb), w in zip(q_arrays, q_widths):
        in_specs.append(pl.BlockSpec((1, tq, w), lambda i, t, cb=cb: (i, t, cb)))
        args.append(arr)
    for (arr, cb), w in zip(kv_arrays, kv_widths):
        in_specs.append(pl.BlockSpec((1, arr.shape[1], w), lambda i, t, cb=cb: (i, 0, cb)))
        args.append(arr)
    for arr in (mk, mv):
        in_specs.append(pl.BlockSpec((1, N_MEM, C_W), lambda i, t: (i, 0, 0)))
        args.append(arr)
    for arr in (bias, mask_a, sink, sel_a, sel_b):
        in_specs.append(_const_spec(arr.shape))
        args.append(arr)
    return pl.pallas_call(
        functools.partial(_attn_kernel, n_chunks=n_chunks, off_a=off_a, off_b=off_b),
        grid=(b, sq // tq),
        in_specs=in_specs,
        out_specs=pl.BlockSpec((1, tq, D_MODEL), lambda i, t: (i, t, 0)),
        out_shape=jax.ShapeDtypeStruct((b, sq, D_MODEL), BF16),
        compiler_params=_params("parallel", "arbitrary"),
        name="attn",
    )(*args)


def _merge_kernel(h_ref, y_ref, x_ref, wgate_ref, bgate_ref, wa_ref, wb_ref, wc_ref, wout_ref, o_ref):
    h = h_ref[...]
    merged = None
    col = 0
    for i, (w_ref, width) in enumerate(((wa_ref, A_Q), (wb_ref, B_W), (wc_ref, C_W))):
        gate = _sigmoid(_dot(h, wgate_ref[:, i * D_MODEL:(i + 1) * D_MODEL])
                        + bgate_ref[:, i * D_MODEL:(i + 1) * D_MODEL])
        term = gate * _dot(y_ref[:, col:col + width], w_ref[...])
        merged = term if merged is None else merged + term
        col += width
    o_ref[...] = x_ref[...] + _dot(merged.astype(BF16), wout_ref[...])


def _merge(h, y, x, w_gate, b_gate, wa, wb, wc, w_out):
    n = h.shape[0]
    tile = pl.BlockSpec((TOKEN_TILE, D_MODEL), lambda i: (i, 0))
    consts = (w_gate, b_gate, wa, wb, wc, w_out)
    return pl.pallas_call(
        _merge_kernel,
        grid=(n // TOKEN_TILE,),
        in_specs=[tile, tile, tile] + [_const_spec(c.shape) for c in consts],
        out_specs=tile,
        out_shape=jax.ShapeDtypeStruct((n, D_MODEL), F32),
        compiler_params=_params("parallel"),
        name="merge",
    )(h, y, x, *consts)


def _rope_tables(pos):
    half = HEAD_DIM // 2
    inv = ROPE_THETA ** (-jnp.arange(half, dtype=F32) / half)
    ang = pos.astype(F32)[:, None] * inv[None, :]
    cos, sin = jnp.cos(ang), jnp.sin(ang)
    reps = V7X_LANES // HEAD_DIM
    return jnp.tile(jnp.concatenate([cos, cos], -1), (1, reps)), jnp.tile(jnp.concatenate([-sin, sin], -1), (1, reps))


def _bias_tables(rel_bias):
    n = B_KEYS
    ext = jnp.pad(rel_bias.astype(F32), ((0, 0), (n - 1 - REL_CLIP, n - 1 - REL_CLIP)), mode="edge")
    ring = jnp.concatenate([ext[:, n - 1::-1], jnp.zeros((B_HEADS, 1), F32), ext[:, :n - 1:-1]], axis=1)
    period = 2 * n
    toeplitz = jnp.tile(ring, (1, n))[:, :n * (period - 1)].reshape(B_HEADS, n, period - 1)[:, :, :n]
    tab = toeplitz.reshape(B_HEADS, B_PREV_CHUNKS + 1, CHUNK, n).transpose(1, 0, 2, 3)
    v = jnp.arange(B_PREV_CHUNKS + 1)[:, None, None, None]
    k_chunk = (jnp.arange(n) // CHUNK)[None, None, None, :]
    return jnp.where(k_chunk <= v, tab, NEG).reshape(B_PREV_CHUNKS + 1, B_HEADS * CHUNK, n)


def _mask_a_tables():
    j = jnp.arange(A_KEYS)[None, :]
    rows = [jnp.where((j // CHUNK) <= v, 0.0, NEG) for v in range(A_PREV_CHUNKS + 1)]
    return jnp.stack([jnp.broadcast_to(r, (8, A_KEYS)) for r in rows]).astype(F32)


def _block_diag_ones():
    i = jnp.arange(V7X_MXU_DIM)
    return (i[:, None] // HEAD_DIM == i[None, :] // HEAD_DIM).astype(BF16)


def _lane_selectors():
    lane = jnp.arange(V7X_LANES)[None, :] // HEAD_DIM
    sel_a = jnp.stack([jnp.broadcast_to(lane == p, (CHUNK, V7X_LANES)) for p in range(2)]).astype(BF16)
    lane = jnp.arange(B_W)[None, :] // HEAD_DIM
    sel_b = jnp.stack([jnp.broadcast_to(lane == h, (CHUNK, B_W)) for h in range(B_HEADS)]).astype(BF16)
    return sel_a, sel_b


def kernel(x_prompt, x_sample, cache_a_k, cache_a_v, cache_b_k, cache_b_v, cache_mem_k, cache_mem_v,
           mem_prompt, g_ff1, w_ff1_gate, w_ff1_up, w_ff1_down, g_mix, w_in, g_qa, g_ka, sinks_a,
           g_qb, g_kb, rel_bias_b, g_qc, g_mem, w_mem_kv, g_kc, w_gate, b_gate, w_br_a, w_br_b,
           w_br_c, w_out, g_ff2, w_ff2_gate, w_ff2_up, w_ff2_down, g_final):
    bp, sp, _ = x_prompt.shape
    bs, ss, _ = x_sample.shape
    l = 0
    row = lambda g: g[l].reshape(1, -1).astype(F32)
    bf = lambda w: w[l].astype(BF16)

    order = jnp.array(A_HEAD_ORDER)
    head_cols = (order[:, None] * HEAD_DIM + jnp.arange(HEAD_DIM)[None, :]).reshape(-1)
    w_in_l = w_in[l]
    w_in_bf = jnp.concatenate([w_in_l[:, :A_Q][:, head_cols], w_in_l[:, A_Q:]], axis=1).astype(BF16)
    w_br_a_bf = w_br_a[l][head_cols].astype(BF16)
    scale = HEAD_DIM ** -0.5
    ones = lambda n: jnp.ones((n,), F32)
    gain = jnp.concatenate([
        jnp.tile(g_qa[l], A_Q_HEADS) * scale, jnp.tile(g_ka[l], A_KV_HEADS), ones(A_KV),
        jnp.tile(g_qb[l], B_HEADS) * scale, jnp.tile(g_kb[l], B_HEADS), ones(B_W),
        jnp.tile(g_qc[l], C_HEADS) * scale]).reshape(1, IN_COLS).astype(F32)
    sink = jnp.broadcast_to(jnp.repeat(sinks_a[l][order].astype(F32), CHUNK)[:, None], (A_Q_HEADS * CHUNK, V7X_LANES))
    bias = _bias_tables(rel_bias_b[l])
    mask_a = _mask_a_tables()
    bd = _block_diag_ones()
    sel_a, sel_b = _lane_selectors()
    ff1 = (row(g_ff1), bf(w_ff1_gate), bf(w_ff1_up), bf(w_ff1_down), row(g_mix))
    ff2 = (row(g_ff2), bf(w_ff2_gate), bf(w_ff2_up), bf(w_ff2_down), row(g_final))
    merge_w = (bf(w_gate), row(b_gate), w_br_a_bf, bf(w_br_b), bf(w_br_c), bf(w_out))

    def trunk(x, pos, batch_view, attend):
        n = x.shape[0] * x.shape[1]
        xf = x.reshape(n, D_MODEL)
        x1, h = _ffn(xf, *ff1, emit_x=True, norm_dtype=BF16)
        cos, sin = _rope_tables(pos)
        qkv, kv32 = _proj(h.reshape(batch_view + (D_MODEL,)), w_in_bf, gain, cos, sin, bd)
        y = attend(qkv)
        x2 = _merge(h, y.reshape(n, D_MODEL), x1, *merge_w)
        (out,) = _ffn(x2, *ff2, emit_x=False, norm_dtype=F32)
        return out.reshape(x.shape), kv32

    def split_kv(kv32, b, rows):
        kv32 = kv32.reshape(b, rows, KV_COLS)
        ka = kv32[..., :A_KV].reshape(b, rows, A_KV_HEADS, HEAD_DIM)
        va = kv32[..., A_KV:2 * A_KV].reshape(b, rows, A_KV_HEADS, HEAD_DIM)
        kb = kv32[..., 2 * A_KV:2 * A_KV + B_W].reshape(b, rows, B_HEADS, HEAD_DIM)
        vb = kv32[..., 2 * A_KV + B_W:].reshape(b, rows, B_HEADS, HEAD_DIM)
        return ka, va, kb, vb

    g_kc_row = jnp.tile(g_kc[l], C_HEADS).reshape(1, C_W).astype(F32)
    mem32, mem16 = _memkv(mem_prompt.reshape(bp * N_MEM, D_MODEL), row(g_mem), bf(w_mem_kv), g_kc_row, bd)
    mem16 = mem16.reshape(bp, N_MEM, 2 * C_W)
    mk_p = mem32[:, :C_W].reshape(1, bp, N_MEM, C_HEADS, HEAD_DIM)
    mv_p = mem32[:, C_W:].reshape(1, bp, N_MEM, C_HEADS, HEAD_DIM)

    def attend_prompt(qkv):
        q_arrays = ((qkv, OFF_QA // A_Q), (qkv, OFF_QB // B_W), (qkv, OFF_QC // C_W))
        kv_arrays = ((qkv, OFF_KA // A_KV), (qkv, OFF_VA // A_KV), (qkv, OFF_KB // B_W), (qkv, OFF_VB // B_W))
        return _attn(q_arrays, kv_arrays, mem16[..., :C_W], mem16[..., C_W:], bias, mask_a, sink, sel_a, sel_b,
                     sq=sp, n_chunks=TOKEN_TILE // CHUNK, off_a=0, off_b=0)

    y_p, kv32_p = trunk(x_prompt, jnp.arange(sp, dtype=jnp.int32), (bp, sp), attend_prompt)
    ka, va, kb, vb = split_kv(kv32_p, bp, TOKEN_TILE)
    keep_a = min(A_PREV_CHUNKS * CHUNK, sp)
    keep_b = min(B_PREV_CHUNKS * CHUNK, sp)
    prompt_caches = (ka[:, TOKEN_TILE - keep_a:][None], va[:, TOKEN_TILE - keep_a:][None],
                     kb[:, TOKEN_TILE - keep_b:][None], vb[:, TOKEN_TILE - keep_b:][None])

    n_s = bs * ss
    rows_per_tile = TOKEN_TILE // ss
    pos_s = PAST_LEN + jnp.tile(jnp.arange(ss, dtype=jnp.int32), rows_per_tile)
    flat16 = lambda c: c[l].reshape(c.shape[1], c.shape[2], -1).astype(BF16)
    ca_k, ca_v, cb_k, cb_v = flat16(cache_a_k), flat16(cache_a_v), flat16(cache_b_k), flat16(cache_b_v)
    cm_k, cm_v = flat16(cache_mem_k), flat16(cache_mem_v)

    def attend_sample(qkv):
        qkv = qkv.reshape(bs, ss, IN_COLS)
        new = lambda off, w: qkv[..., off:off + w]
        kv_arrays = ((jnp.concatenate([ca_k, new(OFF_KA, A_KV)], 1), 0), (jnp.concatenate([ca_v, new(OFF_VA, A_KV)], 1), 0),
                     (jnp.concatenate([cb_k, new(OFF_KB, B_W)], 1), 0), (jnp.concatenate([cb_v, new(OFF_VB, B_W)], 1), 0))
        q_arrays = ((qkv, OFF_QA // A_Q), (qkv, OFF_QB // B_W), (qkv, OFF_QC // C_W))
        return _attn(q_arrays, kv_arrays, cm_k, cm_v, bias, mask_a, sink, sel_a, sel_b,
                     sq=ss, n_chunks=1, off_a=ca_k.shape[1] // CHUNK, off_b=cb_k.shape[1] // CHUNK)

    y_s, kv32_s = trunk(x_sample, pos_s, (n_s // TOKEN_TILE, TOKEN_TILE), attend_sample)
    ka_s, va_s, kb_s, vb_s = split_kv(kv32_s, bs, ss)

    return (y_p, y_s, *prompt_caches, mk_p, mv_p, ka_s[None], va_s[None], kb_s[None], vb_s[None])
```

```python
import functools

import jax
import jax.numpy as jnp
from jax import lax
from jax.experimental import pallas as pl
from jax.experimental.pallas import tpu as pltpu

D_MODEL = 1024
PAST_LEN = 1024
CHUNK = 64
HEAD_DIM = 64
A_Q_HEADS = 8
A_KV_HEADS = 2
A_GROUP = A_Q_HEADS // A_KV_HEADS
A_PREV_CHUNKS = 2
B_HEADS = 4
B_PREV_CHUNKS = 8
REL_CLIP = 128
C_HEADS = 4
N_MEM = 256
FF_DIM = 2816
ROPE_THETA = 10000.0
EPS = 1e-6
NEG = -1e30

A_Q = A_Q_HEADS * HEAD_DIM
A_KV = A_KV_HEADS * HEAD_DIM
B_W = B_HEADS * HEAD_DIM
C_W = C_HEADS * HEAD_DIM
IN_COLS = A_Q + 2 * A_KV + 3 * B_W + C_W
A_KEYS = (A_PREV_CHUNKS + 1) * CHUNK
B_KEYS = (B_PREV_CHUNKS + 1) * CHUNK
OFF_QA, OFF_KA, OFF_VA = 0, A_Q, A_Q + A_KV
OFF_QB, OFF_KB, OFF_VB = A_Q + 2 * A_KV, A_Q + 2 * A_KV + B_W, A_Q + 2 * A_KV + 2 * B_W
OFF_QC = A_Q + 2 * A_KV + 3 * B_W
KV_COLS = 2 * A_KV + 2 * B_W
A_HEAD_ORDER = (0, 4, 1, 5, 2, 6, 3, 7)

V7X_LANES = 128
V7X_MXU_DIM = 256
V7X_VMEM_LIMIT = 56 * 1024 * 1024

TOKEN_TILE = 512
FF_SPLIT = 2
ATTN_CHUNKS_PER_STEP = 32
LOG2E = 1.4426950408889634
BF16 = jnp.bfloat16
F32 = jnp.float32


def _dot(a, b):
    return jnp.dot(a, b, preferred_element_type=F32)


def _dot_nt(a, b):
    return lax.dot_general(a, b, (((1,), (1,)), ((), ())), preferred_element_type=F32)


def _rmsnorm(x, g):
    return x * lax.rsqrt(jnp.mean(x * x, axis=-1, keepdims=True) + EPS) * g


def _sigmoid(z):
    return 1.0 / (1.0 + jnp.exp(-z))


def _const_spec(shape):
    nd = len(shape)
    return pl.BlockSpec(shape, lambda *_: (0,) * nd, pipeline_mode=pl.Buffered(1))


def _params(*sem):
    return pltpu.CompilerParams(dimension_semantics=sem, vmem_limit_bytes=V7X_VMEM_LIMIT)


def _swiglu_residual(x, g_pre, wg_ref, wu_ref, wd_ref):
    hn = _rmsnorm(x, g_pre).astype(BF16)
    fc = FF_DIM // FF_SPLIT
    acc = None
    for c in range(FF_SPLIT):
        g = _dot(hn, wg_ref[:, c * fc:(c + 1) * fc])
        u = _dot(hn, wu_ref[:, c * fc:(c + 1) * fc])
        a = (g * _sigmoid(g) * u).astype(BF16)
        d = _dot(a, wd_ref[c * fc:(c + 1) * fc, :])
        acc = d if acc is None else acc + d
    return x + 0.5 * acc


def _ffn_kernel(x_ref, gpre_ref, wg_ref, wu_ref, wd_ref, gpost_ref, *out_refs, emit_x):
    y = _swiglu_residual(x_ref[...], gpre_ref[...], wg_ref, wu_ref, wd_ref)
    if emit_x:
        out_refs[0][...] = y
    n_ref = out_refs[-1]
    n_ref[...] = _rmsnorm(y, gpost_ref[...]).astype(n_ref.dtype)


def _ffn(x, g_pre, wg, wu, wd, g_post, *, emit_x, norm_dtype):
    n = x.shape[0]
    tile = pl.BlockSpec((TOKEN_TILE, D_MODEL), lambda i: (i, 0))
    out_shape = [jax.ShapeDtypeStruct((n, D_MODEL), norm_dtype)]
    out_specs = [tile]
    if emit_x:
        out_shape.insert(0, jax.ShapeDtypeStruct((n, D_MODEL), F32))
        out_specs.insert(0, tile)
    return pl.pallas_call(
        functools.partial(_ffn_kernel, emit_x=emit_x),
        grid=(n // TOKEN_TILE,),
        in_specs=[tile, _const_spec((1, D_MODEL)), _const_spec(wg.shape), _const_spec(wu.shape),
                  _const_spec(wd.shape), _const_spec((1, D_MODEL))],
        out_specs=out_specs,
        out_shape=out_shape,
        compiler_params=_params("parallel"),
        name="ffn_x" if emit_x else "ffn_final",
    )(x, g_pre, wg, wu, wd, g_post)


def _head_inv_rms(y, bd):
    w = y.shape[1]
    ssq = _dot((y * y).astype(BF16), bd[:w, :w])
    return lax.rsqrt(ssq * (1.0 / HEAD_DIM) + EPS)


def _rope(x, cos, sin_signed, first_half):
    rot = jnp.where(first_half, pltpu.roll(x, V7X_LANES - HEAD_DIM // 2, 1), pltpu.roll(x, HEAD_DIM // 2, 1))
    return x * cos + rot * sin_signed


def _proj_kernel(h_ref, w_ref, gain_ref, cos_ref, sin_ref, bd_ref, qkv_ref, kv32_ref, *, n_tiles):
    y = _dot(h_ref[0], w_ref[...])
    bd = bd_ref[...]
    cos = cos_ref[...]
    sin = sin_ref[...]
    lane = lax.broadcasted_iota(jnp.int32, (TOKEN_TILE, V7X_LANES), 1)
    first_half = (lane % HEAD_DIM) < (HEAD_DIM // 2)

    def normed(off, width):
        blk = y[:, off:off + width]
        return blk * _head_inv_rms(blk, bd) * gain_ref[:, off:off + width]

    pieces = {}
    for off in range(OFF_QA, OFF_QA + A_Q, V7X_MXU_DIM):
        n = normed(off, V7X_MXU_DIM)
        for s in range(0, V7X_MXU_DIM, V7X_LANES):
            pieces[off + s] = _rope(n[:, s:s + V7X_LANES], cos, sin, first_half)
    pieces[OFF_KA] = _rope(normed(OFF_KA, A_KV), cos, sin, first_half)
    pieces[OFF_VA] = y[:, OFF_VA:OFF_VA + A_KV]
    for off in (OFF_QB, OFF_KB, OFF_QC):
        n = normed(off, V7X_MXU_DIM)
        for s in range(0, V7X_MXU_DIM, V7X_LANES):
            pieces[off + s] = n[:, s:s + V7X_LANES]
    for s in range(0, B_W, V7X_LANES):
        pieces[OFF_VB + s] = y[:, OFF_VB + s:OFF_VB + s + V7X_LANES]

    for off, val in pieces.items():
        qkv_ref[0, :, off:off + V7X_LANES] = val.astype(BF16)

    def write_kv32():
        col = 0
        for off, width in ((OFF_KA, A_KV), (OFF_VA, A_KV), (OFF_KB, B_W), (OFF_VB, B_W)):
            for s in range(0, width, V7X_LANES):
                kv32_ref[0, :, col:col + V7X_LANES] = pieces[off + s]
                col += V7X_LANES

    if n_tiles == 1:
        write_kv32()
    else:
        pl.when(pl.program_id(1) == n_tiles - 1)(write_kv32)


def _proj(h, w_in, gain, cos, sin, bd):
    b, s, _ = h.shape
    return pl.pallas_call(
        functools.partial(_proj_kernel, n_tiles=s // TOKEN_TILE),
        grid=(b, s // TOKEN_TILE),
        in_specs=[pl.BlockSpec((1, TOKEN_TILE, D_MODEL), lambda i, t: (i, t, 0)),
                  _const_spec(w_in.shape), _const_spec(gain.shape),
                  pl.BlockSpec((TOKEN_TILE, V7X_LANES), lambda i, t: (t, 0)),
                  pl.BlockSpec((TOKEN_TILE, V7X_LANES), lambda i, t: (t, 0)),
                  _const_spec(bd.shape)],
        out_specs=[pl.BlockSpec((1, TOKEN_TILE, IN_COLS), lambda i, t: (i, t, 0)),
                   pl.BlockSpec((1, TOKEN_TILE, KV_COLS), lambda i, t: (i, 0, 0))],
        out_shape=[jax.ShapeDtypeStruct((b, s, IN_COLS), BF16),
                   jax.ShapeDtypeStruct((b, TOKEN_TILE, KV_COLS), F32)],
        compiler_params=_params("parallel", "arbitrary"),
        name="proj",
    )(h, w_in, gain, cos, sin, bd)


def _memkv_kernel(m_ref, gmem_ref, w_ref, gkc_ref, bd_ref, o32_ref, o16_ref):
    hn = _rmsnorm(m_ref[...], gmem_ref[...]).astype(BF16)
    y = _dot(hn, w_ref[...])
    k = y[:, :C_W]
    k = k * _head_inv_rms(k, bd_ref[...]) * gkc_ref[...]
    v = y[:, C_W:]
    o32_ref[:, :C_W] = k
    o32_ref[:, C_W:] = v
    o16_ref[:, :C_W] = k.astype(BF16)
    o16_ref[:, C_W:] = v.astype(BF16)


def _memkv(mem, g_mem, w, g_kc, bd):
    n = mem.shape[0]
    return pl.pallas_call(
        _memkv_kernel,
        grid=(n // TOKEN_TILE,),
        in_specs=[pl.BlockSpec((TOKEN_TILE, D_MODEL), lambda i: (i, 0)), _const_spec((1, D_MODEL)),
                  _const_spec(w.shape), _const_spec((1, C_W)), _const_spec(bd.shape)],
        out_specs=[pl.BlockSpec((TOKEN_TILE, 2 * C_W), lambda i: (i, 0))] * 2,
        out_shape=[jax.ShapeDtypeStruct((n, 2 * C_W), F32), jax.ShapeDtypeStruct((n, 2 * C_W), BF16)],
        compiler_params=_params("parallel"),
        name="memkv",
    )(mem, g_mem, w, g_kc, bd)


A_ROWS = A_Q_HEADS * CHUNK
B_ROWS = B_HEADS * CHUNK
C_ROWS = C_HEADS * CHUNK


def _attn_kernel(qa_ref, qb_ref, qc_ref, ka_ref, va_ref, kb_ref, vb_ref, mk_ref, mv_ref,
                 bias_ref, maska_ref, sink_ref, sela_ref, selb_ref, y_ref,
                 sa_ref, sb_ref, sc_ref, pa_ref, pb_ref, pc_ref, la_ref, lb_ref, lc_ref,
                 *, n_chunks, off_a, off_b):
    t = pl.program_id(1)
    lane_a = lax.broadcasted_iota(jnp.int32, (CHUNK, V7X_LANES), 1)
    low_half = lane_a < HEAD_DIM
    lane_b = lax.broadcasted_iota(jnp.int32, (CHUNK, B_W), 1) // HEAD_DIM

    def stack_heads(q, sel_ref, n_heads):
        return jnp.concatenate([q * sel_ref[h] for h in range(n_heads)], axis=0)

    def unstack_heads(r, n_heads):
        out = jnp.where(lane_b == 0, r[0:CHUNK], 0.0)
        for h in range(1, n_heads):
            out = out + jnp.where(lane_b == h, r[h * CHUNK:(h + 1) * CHUNK], 0.0)
        return out

    def rows_of(j):
        return pl.ds(pl.multiple_of(j * CHUNK, CHUNK), CHUNK)

    def band(j, off, prev_chunks):
        c = t * n_chunks + j + off
        return pl.multiple_of(jnp.maximum(c - prev_chunks, 0) * CHUNK, CHUNK), jnp.minimum(c, prev_chunks)

    def scores(j, slot):
        rows = rows_of(j)
        start_a, var_a = band(j, off_a, A_PREV_CHUNKS)
        qa = qa_ref[0, rows, :]
        lhs = jnp.concatenate(
            [qa[:, (p // 2) * V7X_LANES:(p // 2 + 1) * V7X_LANES] * sela_ref[p % 2] for p in range(A_Q_HEADS)],
            axis=0)
        sa_ref[slot] = _dot_nt(lhs, ka_ref[0, pl.ds(start_a, A_KEYS), :]) + maska_ref[var_a][0:1]
        start_b, var_b = band(j, off_b, B_PREV_CHUNKS)
        lhs = stack_heads(qb_ref[0, rows, :], selb_ref, B_HEADS)
        sb_ref[slot] = _dot_nt(lhs, kb_ref[0, pl.ds(start_b, B_KEYS), :]) + bias_ref[var_b]
        lhs = stack_heads(qc_ref[0, rows, :], selb_ref, C_HEADS)
        sc_ref[slot] = _dot_nt(lhs, mk_ref[0])

    def numerators(slot):
        for s_ref, p_ref, l_ref, sink in ((sa_ref, pa_ref, la_ref, sink_ref[:, 0:1]),
                                          (sb_ref, pb_ref, lb_ref, None), (sc_ref, pc_ref, lc_ref, None)):
            s = s_ref[slot]
            m = jnp.max(s, axis=-1, keepdims=True)
            if sink is not None:
                m = jnp.maximum(m, sink)
            e = jnp.exp2(s - m)
            l = jnp.sum(e, axis=-1, keepdims=True)
            if sink is not None:
                l = l + jnp.exp2(sink - m)
            p_ref[slot] = e.astype(BF16)
            l_ref[slot] = jnp.broadcast_to(1.0 / l, l_ref.shape[1:])

    def outputs(j, slot):
        rows = rows_of(j)
        start_a, _ = band(j, off_a, A_PREV_CHUNKS)
        r = _dot(pa_ref[slot], va_ref[0, pl.ds(start_a, A_KEYS), :]) * la_ref[slot]
        for jc in range(A_Q // V7X_LANES):
            ev = r[(2 * jc) * CHUNK:(2 * jc + 1) * CHUNK]
            od = r[(2 * jc + 1) * CHUNK:(2 * jc + 2) * CHUNK]
            y_ref[0, rows, jc * V7X_LANES:(jc + 1) * V7X_LANES] = jnp.where(low_half, ev, od).astype(BF16)
        start_b, _ = band(j, off_b, B_PREV_CHUNKS)
        inv_l = lb_ref[slot]
        r = _dot(pb_ref[slot], vb_ref[0, pl.ds(start_b, B_KEYS), :]) * jnp.concatenate([inv_l, inv_l], axis=1)
        y_ref[0, rows, A_Q:A_Q + B_W] = unstack_heads(r, B_HEADS).astype(BF16)
        inv_l = lc_ref[slot]
        r = _dot(pc_ref[slot], mv_ref[0]) * jnp.concatenate([inv_l, inv_l], axis=1)
        y_ref[0, rows, A_Q + B_W:] = unstack_heads(r, C_HEADS).astype(BF16)

    if n_chunks < 4:
        for j in range(n_chunks):
            scores(j, 0)
            numerators(0)
            outputs(j, 0)
        return

    assert n_chunks % 2 == 0
    scores(0, 0)
    scores(1, 1)
    numerators(0)

    def pair(i, carry):
        j = 2 * i
        scores(j, 0)
        numerators(1)
        outputs(j - 2, 0)
        scores(j + 1, 1)
        numerators(0)
        outputs(j - 1, 1)
        return carry

    lax.fori_loop(1, n_chunks // 2, pair, 0)
    numerators(1)
    outputs(n_chunks - 2, 0)
    outputs(n_chunks - 1, 1)


def _attn(q_arrays, kv_arrays, mk, mv, bias, mask_a, sink, sel_a, sel_b, *, sq, n_chunks, off_a, off_b):
    b = q_arrays[0][0].shape[0]
    tq = n_chunks * CHUNK
    q_widths = (A_Q, B_W, C_W)
    kv_widths = (A_KV, A_KV, B_W, B_W)
    in_specs, args = [], []
    for (arr, cb), w in zip(q_arrays, q_widths):
        in_specs.append(pl.BlockSpec((1, tq, w), lambda i, t, cb=cb: (i, t, cb)))
        args.append(arr)
    for (arr, cb), w in zip(kv_arrays, kv_widths):
        in_specs.append(pl.BlockSpec((1, arr.shape[1], w), lambda i, t, cb=cb: (i, 0, cb)))
        args.append(arr)
    for arr in (mk, mv):
        in_specs.append(pl.BlockSpec((1, N_MEM, C_W), lambda i, t: (i, 0, 0)))
        args.append(arr)
    for arr in (bias, mask_a, sink, sel_a, sel_b):
        in_specs.append(_const_spec(arr.shape))
        args.append(arr)
    return pl.pallas_call(
        functools.partial(_attn_kernel, n_chunks=n_chunks, off_a=off_a, off_b=off_b),
        grid=(b, sq // tq),
        in_specs=in_specs,
        out_specs=pl.BlockSpec((1, tq, D_MODEL), lambda i, t: (i, t, 0)),
        out_shape=jax.ShapeDtypeStruct((b, sq, D_MODEL), BF16),
        scratch_shapes=[pltpu.VMEM((2, A_ROWS, A_KEYS), F32), pltpu.VMEM((2, B_ROWS, B_KEYS), F32),
                        pltpu.VMEM((2, C_ROWS, N_MEM), F32),
                        pltpu.VMEM((2, A_ROWS, A_KEYS), BF16), pltpu.VMEM((2, B_ROWS, B_KEYS), BF16),
                        pltpu.VMEM((2, C_ROWS, N_MEM), BF16),
                        pltpu.VMEM((2, A_ROWS, V7X_LANES), F32), pltpu.VMEM((2, B_ROWS, V7X_LANES), F32),
                        pltpu.VMEM((2, C_ROWS, V7X_LANES), F32)],
        compiler_params=_params("parallel", "arbitrary"),
        name="attn",
    )(*args)


def _merge_kernel(h_ref, y_ref, x_ref, wgate_ref, bgate_ref, wa_ref, wb_ref, wc_ref, wout_ref, o_ref):
    h = h_ref[...]
    merged = None
    col = 0
    for i, (w_ref, width) in enumerate(((wa_ref, A_Q), (wb_ref, B_W), (wc_ref, C_W))):
        gate = _sigmoid(_dot(h, wgate_ref[:, i * D_MODEL:(i + 1) * D_MODEL])
                        + bgate_ref[:, i * D_MODEL:(i + 1) * D_MODEL])
        term = gate * _dot(y_ref[:, col:col + width], w_ref[...])
        merged = term if merged is None else merged + term
        col += width
    o_ref[...] = x_ref[...] + _dot(merged.astype(BF16), wout_ref[...])


def _merge(h, y, x, w_gate, b_gate, wa, wb, wc, w_out):
    n = h.shape[0]
    tile = pl.BlockSpec((TOKEN_TILE, D_MODEL), lambda i: (i, 0))
    consts = (w_gate, b_gate, wa, wb, wc, w_out)
    return pl.pallas_call(
        _merge_kernel,
        grid=(n // TOKEN_TILE,),
        in_specs=[tile, tile, tile] + [_const_spec(c.shape) for c in consts],
        out_specs=tile,
        out_shape=jax.ShapeDtypeStruct((n, D_MODEL), F32),
        compiler_params=_params("parallel"),
        name="merge",
    )(h, y, x, *consts)


def _rope_tables(pos):
    half = HEAD_DIM // 2
    inv = ROPE_THETA ** (-jnp.arange(half, dtype=F32) / half)
    ang = pos.astype(F32)[:, None] * inv[None, :]
    cos, sin = jnp.cos(ang), jnp.sin(ang)
    reps = V7X_LANES // HEAD_DIM
    return jnp.tile(jnp.concatenate([cos, cos], -1), (1, reps)), jnp.tile(jnp.concatenate([-sin, sin], -1), (1, reps))


def _bias_tables(rel_bias):
    n = B_KEYS
    ext = jnp.pad(rel_bias.astype(F32), ((0, 0), (n - 1 - REL_CLIP, n - 1 - REL_CLIP)), mode="edge")
    ring = jnp.concatenate([ext[:, n - 1::-1], jnp.zeros((B_HEADS, 1), F32), ext[:, :n - 1:-1]], axis=1)
    period = 2 * n
    toeplitz = jnp.tile(ring, (1, n))[:, :n * (period - 1)].reshape(B_HEADS, n, period - 1)[:, :, :n]
    tab = toeplitz.reshape(B_HEADS, B_PREV_CHUNKS + 1, CHUNK, n).transpose(1, 0, 2, 3)
    v = jnp.arange(B_PREV_CHUNKS + 1)[:, None, None, None]
    k_chunk = (jnp.arange(n) // CHUNK)[None, None, None, :]
    return jnp.where(k_chunk <= v, tab, NEG).reshape(B_PREV_CHUNKS + 1, B_HEADS * CHUNK, n)


def _mask_a_tables():
    j = jnp.arange(A_KEYS)[None, :]
    rows = [jnp.where((j // CHUNK) <= v, 0.0, NEG) for v in range(A_PREV_CHUNKS + 1)]
    return jnp.stack([jnp.broadcast_to(r, (8, A_KEYS)) for r in rows]).astype(F32)


def _block_diag_ones():
    i = jnp.arange(V7X_MXU_DIM)
    return (i[:, None] // HEAD_DIM == i[None, :] // HEAD_DIM).astype(BF16)


def _lane_selectors():
    lane = jnp.arange(V7X_LANES)[None, :] // HEAD_DIM
    sel_a = jnp.stack([jnp.broadcast_to(lane == p, (CHUNK, V7X_LANES)) for p in range(2)]).astype(BF16)
    lane = jnp.arange(B_W)[None, :] // HEAD_DIM
    sel_b = jnp.stack([jnp.broadcast_to(lane == h, (CHUNK, B_W)) for h in range(B_HEADS)]).astype(BF16)
    return sel_a, sel_b


def kernel(x_prompt, x_sample, cache_a_k, cache_a_v, cache_b_k, cache_b_v, cache_mem_k, cache_mem_v,
           mem_prompt, g_ff1, w_ff1_gate, w_ff1_up, w_ff1_down, g_mix, w_in, g_qa, g_ka, sinks_a,
           g_qb, g_kb, rel_bias_b, g_qc, g_mem, w_mem_kv, g_kc, w_gate, b_gate, w_br_a, w_br_b,
           w_br_c, w_out, g_ff2, w_ff2_gate, w_ff2_up, w_ff2_down, g_final):
    bp, sp, _ = x_prompt.shape
    bs, ss, _ = x_sample.shape
    l = 0
    row = lambda g: g[l].reshape(1, -1).astype(F32)
    bf = lambda w: w[l].astype(BF16)

    order = jnp.array(A_HEAD_ORDER)
    head_cols = (order[:, None] * HEAD_DIM + jnp.arange(HEAD_DIM)[None, :]).reshape(-1)
    w_in_l = w_in[l]
    w_in_bf = jnp.concatenate([w_in_l[:, :A_Q][:, head_cols], w_in_l[:, A_Q:]], axis=1).astype(BF16)
    w_br_a_bf = w_br_a[l][head_cols].astype(BF16)
    scale = HEAD_DIM ** -0.5 * LOG2E
    ones = lambda n: jnp.ones((n,), F32)
    gain = jnp.concatenate([
        jnp.tile(g_qa[l], A_Q_HEADS) * scale, jnp.tile(g_ka[l], A_KV_HEADS), ones(A_KV),
        jnp.tile(g_qb[l], B_HEADS) * scale, jnp.tile(g_kb[l], B_HEADS), ones(B_W),
        jnp.tile(g_qc[l], C_HEADS) * scale]).reshape(1, IN_COLS).astype(F32)
    sink = jnp.broadcast_to(jnp.repeat(sinks_a[l][order].astype(F32) * LOG2E, CHUNK)[:, None], (A_ROWS, V7X_LANES))
    bias = _bias_tables(rel_bias_b[l] * LOG2E)
    mask_a = _mask_a_tables()
    bd = _block_diag_ones()
    sel_a, sel_b = _lane_selectors()
    ff1 = (row(g_ff1), bf(w_ff1_gate), bf(w_ff1_up), bf(w_ff1_down), row(g_mix))
    ff2 = (row(g_ff2), bf(w_ff2_gate), bf(w_ff2_up), bf(w_ff2_down), row(g_final))
    merge_w = (bf(w_gate), row(b_gate), w_br_a_bf, bf(w_br_b), bf(w_br_c), bf(w_out))

    def trunk(x, pos, batch_view, attend):
        n = x.shape[0] * x.shape[1]
        xf = x.reshape(n, D_MODEL)
        x1, h = _ffn(xf, *ff1, emit_x=True, norm_dtype=BF16)
        cos, sin = _rope_tables(pos)
        qkv, kv32 = _proj(h.reshape(batch_view + (D_MODEL,)), w_in_bf, gain, cos, sin, bd)
        y = attend(qkv)
        x2 = _merge(h, y.reshape(n, D_MODEL), x1, *merge_w)
        (out,) = _ffn(x2, *ff2, emit_x=False, norm_dtype=F32)
        return out.reshape(x.shape), kv32

    def split_kv(kv32, b, rows):
        kv32 = kv32.reshape(b, rows, KV_COLS)
        ka = kv32[..., :A_KV].reshape(b, rows, A_KV_HEADS, HEAD_DIM)
        va = kv32[..., A_KV:2 * A_KV].reshape(b, rows, A_KV_HEADS, HEAD_DIM)
        kb = kv32[..., 2 * A_KV:2 * A_KV + B_W].reshape(b, rows, B_HEADS, HEAD_DIM)
        vb = kv32[..., 2 * A_KV + B_W:].reshape(b, rows, B_HEADS, HEAD_DIM)
        return ka, va, kb, vb

    g_kc_row = jnp.tile(g_kc[l], C_HEADS).reshape(1, C_W).astype(F32)
    mem32, mem16 = _memkv(mem_prompt.reshape(bp * N_MEM, D_MODEL), row(g_mem), bf(w_mem_kv), g_kc_row, bd)
    mem16 = mem16.reshape(bp, N_MEM, 2 * C_W)
    mk_p = mem32[:, :C_W].reshape(1, bp, N_MEM, C_HEADS, HEAD_DIM)
    mv_p = mem32[:, C_W:].reshape(1, bp, N_MEM, C_HEADS, HEAD_DIM)

    def attend_prompt(qkv):
        q_arrays = ((qkv, OFF_QA // A_Q), (qkv, OFF_QB // B_W), (qkv, OFF_QC // C_W))
        kv_arrays = ((qkv, OFF_KA // A_KV), (qkv, OFF_VA // A_KV), (qkv, OFF_KB // B_W), (qkv, OFF_VB // B_W))
        return _attn(q_arrays, kv_arrays, mem16[..., :C_W], mem16[..., C_W:], bias, mask_a, sink, sel_a, sel_b,
                     sq=sp, n_chunks=min(ATTN_CHUNKS_PER_STEP, sp // CHUNK), off_a=0, off_b=0)

    y_p, kv32_p = trunk(x_prompt, jnp.arange(sp, dtype=jnp.int32), (bp, sp), attend_prompt)
    ka, va, kb, vb = split_kv(kv32_p, bp, TOKEN_TILE)
    keep_a = min(A_PREV_CHUNKS * CHUNK, sp)
    keep_b = min(B_PREV_CHUNKS * CHUNK, sp)
    prompt_caches = (ka[:, TOKEN_TILE - keep_a:][None], va[:, TOKEN_TILE - keep_a:][None],
                     kb[:, TOKEN_TILE - keep_b:][None], vb[:, TOKEN_TILE - keep_b:][None])

    n_s = bs * ss
    rows_per_tile = TOKEN_TILE // ss
    pos_s = PAST_LEN + jnp.tile(jnp.arange(ss, dtype=jnp.int32), rows_per_tile)
    flat16 = lambda c: c[l].reshape(c.shape[1], c.shape[2], -1).astype(BF16)
    ca_k, ca_v, cb_k, cb_v = flat16(cache_a_k), flat16(cache_a_v), flat16(cache_b_k), flat16(cache_b_v)
    cm_k, cm_v = flat16(cache_mem_k), flat16(cache_mem_v)

    def attend_sample(qkv):
        qkv = qkv.reshape(bs, ss, IN_COLS)
        new = lambda off, w: qkv[..., off:off + w]
        kv_arrays = ((jnp.concatenate([ca_k, new(OFF_KA, A_KV)], 1), 0), (jnp.concatenate([ca_v, new(OFF_VA, A_KV)], 1), 0),
                     (jnp.concatenate([cb_k, new(OFF_KB, B_W)], 1), 0), (jnp.concatenate([cb_v, new(OFF_VB, B_W)], 1), 0))
        q_arrays = ((qkv, OFF_QA // A_Q), (qkv, OFF_QB // B_W), (qkv, OFF_QC // C_W))
        return _attn(q_arrays, kv_arrays, cm_k, cm_v, bias, mask_a, sink, sel_a, sel_b,
                     sq=ss, n_chunks=1, off_a=ca_k.shape[1] // CHUNK, off_b=cb_k.shape[1] // CHUNK)

    y_s, kv32_s = trunk(x_sample, pos_s, (n_s // TOKEN_TILE, TOKEN_TILE), attend_sample)
    ka_s, va_s, kb_s, vb_s = split_kv(kv32_s, bs, ss)

    return (y_p, y_s, *prompt_caches, mk_p, mv_p, ka_s[None], va_s[None], kb_s[None], vb_s[None])
```

```python
import functools

import jax
import jax.numpy as jnp
from jax import lax
from jax.experimental import pallas as pl
from jax.experimental.pallas import tpu as pltpu

D_MODEL = 1024
PAST_LEN = 1024
CHUNK = 64
HEAD_DIM = 64
A_Q_HEADS = 8
A_KV_HEADS = 2
A_GROUP = A_Q_HEADS // A_KV_HEADS
A_PREV_CHUNKS = 2
B_HEADS = 4
B_PREV_CHUNKS = 8
REL_CLIP = 128
C_HEADS = 4
N_MEM = 256
FF_DIM = 2816
ROPE_THETA = 10000.0
EPS = 1e-6
NEG = -1e30

A_Q = A_Q_HEADS * HEAD_DIM
A_KV = A_KV_HEADS * HEAD_DIM
B_W = B_HEADS * HEAD_DIM
C_W = C_HEADS * HEAD_DIM
IN_COLS = A_Q + 2 * A_KV + 3 * B_W + C_W
A_KEYS = (A_PREV_CHUNKS + 1) * CHUNK
B_BAND_CHUNKS = B_PREV_CHUNKS + 2
B_KEYS = B_BAND_CHUNKS * CHUNK
OFF_QA, OFF_KA, OFF_VA = 0, A_Q, A_Q + A_KV
OFF_QB, OFF_KB, OFF_VB = A_Q + 2 * A_KV, A_Q + 2 * A_KV + B_W, A_Q + 2 * A_KV + 2 * B_W
OFF_QC = A_Q + 2 * A_KV + 3 * B_W
KV_COLS = 2 * A_KV + 2 * B_W
A_HEAD_ORDER = (0, 4, 1, 5, 2, 6, 3, 7)

V7X_LANES = 128
V7X_MXU_DIM = 256
V7X_VMEM_LIMIT = 56 * 1024 * 1024

TOKEN_TILE = 512
FF_CHUNKS = (6 * V7X_MXU_DIM, 5 * V7X_MXU_DIM)
assert sum(FF_CHUNKS) == FF_DIM
FFN_SUBTILES = 2
PROJ_SUBTILES = 2
MERGE_SUBTILES = 2
ATTN_CHUNKS_PER_STEP = 32
LOG2E = 1.4426950408889634
BF16 = jnp.bfloat16
F32 = jnp.float32


def _dot(a, b):
    return jnp.dot(a, b, preferred_element_type=F32)


def _dot_nt(a, b):
    return lax.dot_general(a, b, (((1,), (1,)), ((), ())), preferred_element_type=F32)


def _rmsnorm(x, g):
    return x * lax.rsqrt(jnp.mean(x * x, axis=-1, keepdims=True) + EPS) * g


def _sigmoid(z):
    return 1.0 / (1.0 + jnp.exp(-z))


def _const_spec(shape):
    nd = len(shape)
    return pl.BlockSpec(shape, lambda *_: (0,) * nd, pipeline_mode=pl.Buffered(1))


def _params(*sem):
    return pltpu.CompilerParams(dimension_semantics=sem, vmem_limit_bytes=V7X_VMEM_LIMIT)


def _swiglu_residual(x, g_pre, wg_ref, wu_ref, wd_ref):
    hn = _rmsnorm(x, g_pre).astype(BF16)
    acc = None
    lo = 0
    for width in FF_CHUNKS:
        g = _dot(hn, wg_ref[:, lo:lo + width])
        u = _dot(hn, wu_ref[:, lo:lo + width])
        a = (g * _sigmoid(g) * u).astype(BF16)
        d = _dot(a, wd_ref[lo:lo + width, :])
        acc = d if acc is None else acc + d
        lo += width
    return x + 0.5 * acc


def _ffn_kernel(x_ref, gpre_ref, wg_ref, wu_ref, wd_ref, gpost_ref, *out_refs, emit_x):
    n_ref = out_refs[-1]
    sub = TOKEN_TILE // FFN_SUBTILES
    for i in range(FFN_SUBTILES):
        rows = slice(i * sub, (i + 1) * sub)
        y = _swiglu_residual(x_ref[rows, :], gpre_ref[...], wg_ref, wu_ref, wd_ref)
        if emit_x:
            out_refs[0][rows, :] = y
        n_ref[rows, :] = _rmsnorm(y, gpost_ref[...]).astype(n_ref.dtype)


def _ffn(x, g_pre, wg, wu, wd, g_post, *, emit_x, norm_dtype):
    n = x.shape[0]
    tile = pl.BlockSpec((TOKEN_TILE, D_MODEL), lambda i: (i, 0))
    out_shape = [jax.ShapeDtypeStruct((n, D_MODEL), norm_dtype)]
    out_specs = [tile]
    if emit_x:
        out_shape.insert(0, jax.ShapeDtypeStruct((n, D_MODEL), F32))
        out_specs.insert(0, tile)
    return pl.pallas_call(
        functools.partial(_ffn_kernel, emit_x=emit_x),
        grid=(n // TOKEN_TILE,),
        in_specs=[tile, _const_spec((1, D_MODEL)), _const_spec(wg.shape), _const_spec(wu.shape),
                  _const_spec(wd.shape), _const_spec((1, D_MODEL))],
        out_specs=out_specs,
        out_shape=out_shape,
        compiler_params=_params("parallel"),
        name="ffn_x" if emit_x else "ffn_final",
    )(x, g_pre, wg, wu, wd, g_post)


def _head_inv_rms(y, bd):
    w = y.shape[1]
    ssq = _dot((y * y).astype(BF16), bd[:w, :w])
    return lax.rsqrt(ssq * (1.0 / HEAD_DIM) + EPS)


def _rope(x, cos, sin_signed, first_half):
    rot = jnp.where(first_half, pltpu.roll(x, V7X_LANES - HEAD_DIM // 2, 1), pltpu.roll(x, HEAD_DIM // 2, 1))
    return x * cos + rot * sin_signed


def _proj_kernel(h_ref, w_ref, gain_ref, cos_ref, sin_ref, bd_ref, qkv_ref, kv32_ref):
    bd = bd_ref[...]
    sub = TOKEN_TILE // PROJ_SUBTILES
    lane = lax.broadcasted_iota(jnp.int32, (sub, V7X_LANES), 1)
    first_half = (lane % HEAD_DIM) < (HEAD_DIM // 2)
    for i in range(PROJ_SUBTILES):
        rows = slice(i * sub, (i + 1) * sub)
        y = _dot(h_ref[0, rows, :], w_ref[...])
        cos = cos_ref[rows, :]
        sin = sin_ref[rows, :]

        def normed(off, width):
            blk = y[:, off:off + width]
            return blk * _head_inv_rms(blk, bd) * gain_ref[:, off:off + width]

        pieces = {}
        for off in range(OFF_QA, OFF_QA + A_Q, V7X_MXU_DIM):
            n = normed(off, V7X_MXU_DIM)
            for s in range(0, V7X_MXU_DIM, V7X_LANES):
                pieces[off + s] = _rope(n[:, s:s + V7X_LANES], cos, sin, first_half)
        pieces[OFF_KA] = _rope(normed(OFF_KA, A_KV), cos, sin, first_half)
        pieces[OFF_VA] = y[:, OFF_VA:OFF_VA + A_KV]
        for off in (OFF_QB, OFF_KB, OFF_QC):
            n = normed(off, V7X_MXU_DIM)
            for s in range(0, V7X_MXU_DIM, V7X_LANES):
                pieces[off + s] = n[:, s:s + V7X_LANES]
        for s in range(0, B_W, V7X_LANES):
            pieces[OFF_VB + s] = y[:, OFF_VB + s:OFF_VB + s + V7X_LANES]

        for off, val in pieces.items():
            qkv_ref[0, rows, off:off + V7X_LANES] = val.astype(BF16)
        col = 0
        for off, width in ((OFF_KA, A_KV), (OFF_VA, A_KV), (OFF_KB, B_W), (OFF_VB, B_W)):
            for s in range(0, width, V7X_LANES):
                kv32_ref[0, rows, col:col + V7X_LANES] = pieces[off + s]
                col += V7X_LANES


def _proj(h, w_in, gain, cos, sin, bd):
    b, s, _ = h.shape
    return pl.pallas_call(
        _proj_kernel,
        grid=(b, s // TOKEN_TILE),
        in_specs=[pl.BlockSpec((1, TOKEN_TILE, D_MODEL), lambda i, t: (i, t, 0)),
                  _const_spec(w_in.shape), _const_spec(gain.shape),
                  pl.BlockSpec((TOKEN_TILE, V7X_LANES), lambda i, t: (t, 0)),
                  pl.BlockSpec((TOKEN_TILE, V7X_LANES), lambda i, t: (t, 0)),
                  _const_spec(bd.shape)],
        out_specs=[pl.BlockSpec((1, TOKEN_TILE, IN_COLS), lambda i, t: (i, t, 0)),
                   pl.BlockSpec((1, TOKEN_TILE, KV_COLS), lambda i, t: (i, 0, 0))],
        out_shape=[jax.ShapeDtypeStruct((b, s, IN_COLS), BF16),
                   jax.ShapeDtypeStruct((b, TOKEN_TILE, KV_COLS), F32)],
        compiler_params=_params("parallel", "arbitrary"),
        name="proj",
    )(h, w_in, gain, cos, sin, bd)


def _memkv_kernel(m_ref, gmem_ref, w_ref, gkc_ref, bd_ref, o32_ref, o16_ref):
    hn = _rmsnorm(m_ref[...], gmem_ref[...]).astype(BF16)
    y = _dot(hn, w_ref[...])
    k = y[:, :C_W]
    k = k * _head_inv_rms(k, bd_ref[...]) * gkc_ref[...]
    v = y[:, C_W:]
    o32_ref[:, :C_W] = k
    o32_ref[:, C_W:] = v
    o16_ref[:, :C_W] = k.astype(BF16)
    o16_ref[:, C_W:] = v.astype(BF16)


def _memkv(mem, g_mem, w, g_kc, bd):
    n = mem.shape[0]
    return pl.pallas_call(
        _memkv_kernel,
        grid=(n // TOKEN_TILE,),
        in_specs=[pl.BlockSpec((TOKEN_TILE, D_MODEL), lambda i: (i, 0)), _const_spec((1, D_MODEL)),
                  _const_spec(w.shape), _const_spec((1, C_W)), _const_spec(bd.shape)],
        out_specs=[pl.BlockSpec((TOKEN_TILE, 2 * C_W), lambda i: (i, 0))] * 2,
        out_shape=[jax.ShapeDtypeStruct((n, 2 * C_W), F32), jax.ShapeDtypeStruct((n, 2 * C_W), BF16)],
        compiler_params=_params("parallel"),
        name="memkv",
    )(mem, g_mem, w, g_kc, bd)


A_ROWS = A_Q_HEADS * CHUNK
A_SCORE_COLS = 2 * V7X_LANES
B_ROWS = B_HEADS * CHUNK
C_ROWS = C_HEADS * CHUNK


def _attn_kernel(qa_ref, qb_ref, qc_ref, ka_ref, va_ref, kb_ref, vb_ref, mk_ref, mv_ref,
                 bias_ref, maska_ref, sinkpad_ref, sela_ref, selb_ref, y_ref,
                 sa_ref, sb_ref, sc_ref, pa_ref, pb_ref, pc_ref, la_ref, lb_ref, lc_ref,
                 *, n_chunks, off_a, off_b):
    t = pl.program_id(1)
    lane_a = lax.broadcasted_iota(jnp.int32, (CHUNK, V7X_LANES), 1)
    low_half = lane_a < HEAD_DIM
    lane_b = lax.broadcasted_iota(jnp.int32, (CHUNK, B_W), 1) // HEAD_DIM

    def stack_heads(q, sel_ref, n_heads):
        return jnp.concatenate([q * sel_ref[h] for h in range(n_heads)], axis=0)

    def unstack_heads(r, n_heads):
        out = r[(n_heads - 1) * CHUNK:]
        for h in reversed(range(n_heads - 1)):
            out = jnp.where(lane_b == h, r[h * CHUNK:(h + 1) * CHUNK], out)
        return out

    def rows_of(j):
        return pl.ds(pl.multiple_of(j * CHUNK, CHUNK), CHUNK)

    def band(j, off, prev_chunks):
        c = t * n_chunks + j + off
        return pl.multiple_of(jnp.maximum(c - prev_chunks, 0) * CHUNK, CHUNK), jnp.minimum(c, prev_chunks)

    def scores(j, slot, first_chunks):
        rows = rows_of(j)
        start_a, var_a = band(j, off_a, A_PREV_CHUNKS)
        qa = qa_ref[0, rows, :]
        lhs = jnp.concatenate(
            [qa[:, (p // 2) * V7X_LANES:(p // 2 + 1) * V7X_LANES] * sela_ref[p % 2] for p in range(A_Q_HEADS)],
            axis=0)
        s = _dot_nt(lhs, ka_ref[0, pl.ds(start_a, A_KEYS), :])
        if first_chunks:
            s = s + maska_ref[var_a][0:1]
        sa_ref[slot, :, :A_KEYS] = s
        start_b, var_b = band(j, off_b, B_BAND_CHUNKS - 1)
        lhs = stack_heads(qb_ref[0, rows, :], selb_ref, B_HEADS)
        sb_ref[slot] = _dot_nt(lhs, kb_ref[0, pl.ds(start_b, B_KEYS), :]) + bias_ref[var_b]
        lhs = stack_heads(qc_ref[0, rows, :], selb_ref, C_HEADS)
        sc_ref[slot] = _dot_nt(lhs, mk_ref[0])

    def numerators(slot):
        for s_ref, p_ref, l_ref in ((sa_ref, pa_ref, la_ref), (sb_ref, pb_ref, lb_ref), (sc_ref, pc_ref, lc_ref)):
            s = s_ref[slot]
            e = jnp.exp2(s - jnp.max(s, axis=-1, keepdims=True))
            p_ref[slot] = e.astype(BF16)
            l_ref[slot] = jnp.broadcast_to(1.0 / jnp.sum(e, axis=-1, keepdims=True), l_ref.shape[1:])

    def outputs(j, slot):
        rows = rows_of(j)
        start_a, _ = band(j, off_a, A_PREV_CHUNKS)
        r = _dot(pa_ref[slot, :, :A_KEYS], va_ref[0, pl.ds(start_a, A_KEYS), :]) * la_ref[slot]
        for jc in range(A_Q // V7X_LANES):
            ev = r[(2 * jc) * CHUNK:(2 * jc + 1) * CHUNK]
            od = r[(2 * jc + 1) * CHUNK:(2 * jc + 2) * CHUNK]
            y_ref[0, rows, jc * V7X_LANES:(jc + 1) * V7X_LANES] = jnp.where(low_half, ev, od).astype(BF16)
        start_b, _ = band(j, off_b, B_BAND_CHUNKS - 1)
        inv_l = lb_ref[slot]
        r = _dot(pb_ref[slot], vb_ref[0, pl.ds(start_b, B_KEYS), :]) * jnp.concatenate([inv_l, inv_l], axis=1)
        y_ref[0, rows, A_Q:A_Q + B_W] = unstack_heads(r, B_HEADS).astype(BF16)
        inv_l = lc_ref[slot]
        r = _dot(pc_ref[slot], mv_ref[0]) * jnp.concatenate([inv_l, inv_l], axis=1)
        y_ref[0, rows, A_Q + B_W:] = unstack_heads(r, C_HEADS).astype(BF16)

    for slot in range(2):
        sa_ref[slot, :, A_KEYS:] = sinkpad_ref[...]

    if n_chunks < 4:
        for j in range(n_chunks):
            scores(j, 0, j + off_a < A_PREV_CHUNKS)
            numerators(0)
            outputs(j, 0)
        return

    assert n_chunks % 2 == 0 and A_PREV_CHUNKS <= 2
    scores(0, 0, True)
    scores(1, 1, True)
    numerators(0)

    def pair(i, carry):
        j = 2 * i
        scores(j, 0, False)
        numerators(1)
        outputs(j - 2, 0)
        scores(j + 1, 1, False)
        numerators(0)
        outputs(j - 1, 1)
        return carry

    lax.fori_loop(1, n_chunks // 2, pair, 0)
    numerators(1)
    outputs(n_chunks - 2, 0)
    outputs(n_chunks - 1, 1)


def _attn(q_arrays, kv_arrays, mk, mv, bias, mask_a, sink, sel_a, sel_b, *, sq, n_chunks, off_a, off_b):
    b = q_arrays[0][0].shape[0]
    tq = n_chunks * CHUNK
    q_widths = (A_Q, B_W, C_W)
    kv_widths = (A_KV, A_KV, B_W, B_W)
    in_specs, args = [], []
    for (arr, cb), w in zip(q_arrays, q_widths):
        in_specs.append(pl.BlockSpec((1, tq, w), lambda i, t, cb=cb: (i, t, cb)))
        args.append(arr)
    for (arr, cb), w in zip(kv_arrays, kv_widths):
        in_specs.append(pl.BlockSpec((1, arr.shape[1], w), lambda i, t, cb=cb: (i, 0, cb)))
        args.append(arr)
    for arr in (mk, mv):
        in_specs.append(pl.BlockSpec((1, N_MEM, C_W), lambda i, t: (i, 0, 0)))
        args.append(arr)
    for arr in (bias, mask_a, sink, sel_a, sel_b):
        in_specs.append(_const_spec(arr.shape))
        args.append(arr)
    return pl.pallas_call(
        functools.partial(_attn_kernel, n_chunks=n_chunks, off_a=off_a, off_b=off_b),
        grid=(b, sq // tq),
        in_specs=in_specs,
        out_specs=pl.BlockSpec((1, tq, D_MODEL), lambda i, t: (i, t, 0)),
        out_shape=jax.ShapeDtypeStruct((b, sq, D_MODEL), BF16),
        scratch_shapes=[pltpu.VMEM((2, A_ROWS, A_SCORE_COLS), F32), pltpu.VMEM((2, B_ROWS, B_KEYS), F32),
                        pltpu.VMEM((2, C_ROWS, N_MEM), F32),
                        pltpu.VMEM((2, A_ROWS, A_SCORE_COLS), BF16), pltpu.VMEM((2, B_ROWS, B_KEYS), BF16),
                        pltpu.VMEM((2, C_ROWS, N_MEM), BF16),
                        pltpu.VMEM((2, A_ROWS, V7X_LANES), F32), pltpu.VMEM((2, B_ROWS, V7X_LANES), F32),
                        pltpu.VMEM((2, C_ROWS, V7X_LANES), F32)],
        compiler_params=_params("parallel", "arbitrary"),
        name="attn",
    )(*args)


def _merge_kernel(h_ref, y_ref, x_ref, wgate_ref, bgate_ref, wa_ref, wb_ref, wc_ref, wout_ref, o_ref):
    sub = TOKEN_TILE // MERGE_SUBTILES
    for t in range(MERGE_SUBTILES):
        rows = slice(t * sub, (t + 1) * sub)
        h = h_ref[rows, :]
        merged = None
        col = 0
        for i, (w_ref, width) in enumerate(((wa_ref, A_Q), (wb_ref, B_W), (wc_ref, C_W))):
            gate = _sigmoid(_dot(h, wgate_ref[:, i * D_MODEL:(i + 1) * D_MODEL])
                            + bgate_ref[:, i * D_MODEL:(i + 1) * D_MODEL])
            term = gate * _dot(y_ref[rows, col:col + width], w_ref[...])
            merged = term if merged is None else merged + term
            col += width
        o_ref[rows, :] = x_ref[rows, :] + _dot(merged.astype(BF16), wout_ref[...])


def _merge(h, y, x, w_gate, b_gate, wa, wb, wc, w_out):
    n = h.shape[0]
    tile = pl.BlockSpec((TOKEN_TILE, D_MODEL), lambda i: (i, 0))
    consts = (w_gate, b_gate, wa, wb, wc, w_out)
    return pl.pallas_call(
        _merge_kernel,
        grid=(n // TOKEN_TILE,),
        in_specs=[tile, tile, tile] + [_const_spec(c.shape) for c in consts],
        out_specs=tile,
        out_shape=jax.ShapeDtypeStruct((n, D_MODEL), F32),
        compiler_params=_params("parallel"),
        name="merge",
    )(h, y, x, *consts)


def _rope_tables(pos):
    half = HEAD_DIM // 2
    inv = ROPE_THETA ** (-jnp.arange(half, dtype=F32) / half)
    ang = pos.astype(F32)[:, None] * inv[None, :]
    cos, sin = jnp.cos(ang), jnp.sin(ang)
    reps = V7X_LANES // HEAD_DIM
    return jnp.tile(jnp.concatenate([cos, cos], -1), (1, reps)), jnp.tile(jnp.concatenate([-sin, sin], -1), (1, reps))


def _bias_tables(rel_bias):
    n = B_KEYS
    ext = jnp.pad(rel_bias.astype(F32), ((0, 0), (n - 1 - REL_CLIP, n - 1 - REL_CLIP)), mode="edge")
    ring = jnp.concatenate([ext[:, n - 1::-1], jnp.zeros((B_HEADS, 1), F32), ext[:, :n - 1:-1]], axis=1)
    period = 2 * n
    toeplitz = jnp.tile(ring, (1, n))[:, :n * (period - 1)].reshape(B_HEADS, n, period - 1)[:, :, :n]
    tab = toeplitz.reshape(B_HEADS, B_BAND_CHUNKS, CHUNK, n).transpose(1, 0, 2, 3)
    v = jnp.arange(B_BAND_CHUNKS)[:, None, None, None]
    k_chunk = (jnp.arange(n) // CHUNK)[None, None, None, :]
    visible = (k_chunk <= v) & (v - k_chunk <= B_PREV_CHUNKS)
    return jnp.where(visible, tab, NEG).reshape(B_BAND_CHUNKS, B_HEADS * CHUNK, n)


def _mask_a_tables():
    j = jnp.arange(A_KEYS)[None, :]
    rows = [jnp.where((j // CHUNK) <= v, 0.0, NEG) for v in range(A_PREV_CHUNKS + 1)]
    return jnp.stack([jnp.broadcast_to(r, (8, A_KEYS)) for r in rows]).astype(F32)


def _block_diag_ones():
    i = jnp.arange(V7X_MXU_DIM)
    return (i[:, None] // HEAD_DIM == i[None, :] // HEAD_DIM).astype(BF16)


def _lane_selectors():
    lane = jnp.arange(V7X_LANES)[None, :] // HEAD_DIM
    sel_a = jnp.stack([jnp.broadcast_to(lane == p, (CHUNK, V7X_LANES)) for p in range(2)]).astype(BF16)
    lane = jnp.arange(B_W)[None, :] // HEAD_DIM
    sel_b = jnp.stack([jnp.broadcast_to(lane == h, (CHUNK, B_W)) for h in range(B_HEADS)]).astype(BF16)
    return sel_a, sel_b


def kernel(x_prompt, x_sample, cache_a_k, cache_a_v, cache_b_k, cache_b_v, cache_mem_k, cache_mem_v,
           mem_prompt, g_ff1, w_ff1_gate, w_ff1_up, w_ff1_down, g_mix, w_in, g_qa, g_ka, sinks_a,
           g_qb, g_kb, rel_bias_b, g_qc, g_mem, w_mem_kv, g_kc, w_gate, b_gate, w_br_a, w_br_b,
           w_br_c, w_out, g_ff2, w_ff2_gate, w_ff2_up, w_ff2_down, g_final):
    bp, sp, _ = x_prompt.shape
    bs, ss, _ = x_sample.shape
    l = 0
    row = lambda g: g[l].reshape(1, -1).astype(F32)
    bf = lambda w: w[l].astype(BF16)

    order = jnp.array(A_HEAD_ORDER)
    head_cols = (order[:, None] * HEAD_DIM + jnp.arange(HEAD_DIM)[None, :]).reshape(-1)
    w_in_l = w_in[l]
    w_in_bf = jnp.concatenate([w_in_l[:, :A_Q][:, head_cols], w_in_l[:, A_Q:]], axis=1).astype(BF16)
    w_br_a_bf = w_br_a[l][head_cols].astype(BF16)
    scale = HEAD_DIM ** -0.5 * LOG2E
    ones = lambda n: jnp.ones((n,), F32)
    gain = jnp.concatenate([
        jnp.tile(g_qa[l], A_Q_HEADS) * scale, jnp.tile(g_ka[l], A_KV_HEADS), ones(A_KV),
        jnp.tile(g_qb[l], B_HEADS) * scale, jnp.tile(g_kb[l], B_HEADS), ones(B_W),
        jnp.tile(g_qc[l], C_HEADS) * scale]).reshape(1, IN_COLS).astype(F32)
    sink = jnp.concatenate([jnp.repeat(sinks_a[l][order].astype(F32) * LOG2E, CHUNK)[:, None],
                            jnp.full((A_ROWS, A_SCORE_COLS - A_KEYS - 1), NEG, F32)], axis=1)
    bias = _bias_tables(rel_bias_b[l] * LOG2E)
    mask_a = _mask_a_tables()
    bd = _block_diag_ones()
    sel_a, sel_b = _lane_selectors()
    ff1 = (row(g_ff1), bf(w_ff1_gate), bf(w_ff1_up), bf(w_ff1_down), row(g_mix))
    ff2 = (row(g_ff2), bf(w_ff2_gate), bf(w_ff2_up), bf(w_ff2_down), row(g_final))
    merge_w = (bf(w_gate), row(b_gate), w_br_a_bf, bf(w_br_b), bf(w_br_c), bf(w_out))

    def trunk(x, pos, batch_view, attend):
        n = x.shape[0] * x.shape[1]
        xf = x.reshape(n, D_MODEL)
        x1, h = _ffn(xf, *ff1, emit_x=True, norm_dtype=BF16)
        cos, sin = _rope_tables(pos)
        qkv, kv32 = _proj(h.reshape(batch_view + (D_MODEL,)), w_in_bf, gain, cos, sin, bd)
        y = attend(qkv)
        x2 = _merge(h, y.reshape(n, D_MODEL), x1, *merge_w)
        (out,) = _ffn(x2, *ff2, emit_x=False, norm_dtype=F32)
        return out.reshape(x.shape), kv32

    def split_kv(kv32, b, rows):
        kv32 = kv32.reshape(b, rows, KV_COLS)
        ka = kv32[..., :A_KV].reshape(b, rows, A_KV_HEADS, HEAD_DIM)
        va = kv32[..., A_KV:2 * A_KV].reshape(b, rows, A_KV_HEADS, HEAD_DIM)
        kb = kv32[..., 2 * A_KV:2 * A_KV + B_W].reshape(b, rows, B_HEADS, HEAD_DIM)
        vb = kv32[..., 2 * A_KV + B_W:].reshape(b, rows, B_HEADS, HEAD_DIM)
        return ka, va, kb, vb

    g_kc_row = jnp.tile(g_kc[l], C_HEADS).reshape(1, C_W).astype(F32)
    mem32, mem16 = _memkv(mem_prompt.reshape(bp * N_MEM, D_MODEL), row(g_mem), bf(w_mem_kv), g_kc_row, bd)
    mem16 = mem16.reshape(bp, N_MEM, 2 * C_W)
    mk_p = mem32[:, :C_W].reshape(1, bp, N_MEM, C_HEADS, HEAD_DIM)
    mv_p = mem32[:, C_W:].reshape(1, bp, N_MEM, C_HEADS, HEAD_DIM)

    def attend_prompt(qkv):
        q_arrays = ((qkv, OFF_QA // A_Q), (qkv, OFF_QB // B_W), (qkv, OFF_QC // C_W))
        kv_arrays = ((qkv, OFF_KA // A_KV), (qkv, OFF_VA // A_KV), (qkv, OFF_KB // B_W), (qkv, OFF_VB // B_W))
        return _attn(q_arrays, kv_arrays, mem16[..., :C_W], mem16[..., C_W:], bias, mask_a, sink, sel_a, sel_b,
                     sq=sp, n_chunks=min(ATTN_CHUNKS_PER_STEP, sp // CHUNK), off_a=0, off_b=0)

    y_p, kv32_p = trunk(x_prompt, jnp.arange(sp, dtype=jnp.int32), (bp, sp), attend_prompt)
    ka, va, kb, vb = split_kv(kv32_p, bp, TOKEN_TILE)
    keep_a = min(A_PREV_CHUNKS * CHUNK, sp)
    keep_b = min(B_PREV_CHUNKS * CHUNK, sp)
    prompt_caches = (ka[:, TOKEN_TILE - keep_a:][None], va[:, TOKEN_TILE - keep_a:][None],
                     kb[:, TOKEN_TILE - keep_b:][None], vb[:, TOKEN_TILE - keep_b:][None])

    n_s = bs * ss
    rows_per_tile = TOKEN_TILE // ss
    pos_s = PAST_LEN + jnp.tile(jnp.arange(ss, dtype=jnp.int32), rows_per_tile)
    flat16 = lambda c: c[l].reshape(c.shape[1], c.shape[2], -1).astype(BF16)
    ca_k, ca_v, cb_k, cb_v = flat16(cache_a_k), flat16(cache_a_v), flat16(cache_b_k), flat16(cache_b_v)
    cm_k, cm_v = flat16(cache_mem_k), flat16(cache_mem_v)

    def attend_sample(qkv):
        qkv = qkv.reshape(bs, ss, IN_COLS)
        new = lambda off, w: qkv[..., off:off + w]
        lead = jnp.zeros((bs, B_KEYS - cb_k.shape[1] - ss, B_W), BF16)
        kv_arrays = ((jnp.concatenate([ca_k, new(OFF_KA, A_KV)], 1), 0), (jnp.concatenate([ca_v, new(OFF_VA, A_KV)], 1), 0),
                     (jnp.concatenate([lead, cb_k, new(OFF_KB, B_W)], 1), 0),
                     (jnp.concatenate([lead, cb_v, new(OFF_VB, B_W)], 1), 0))
        q_arrays = ((qkv, OFF_QA // A_Q), (qkv, OFF_QB // B_W), (qkv, OFF_QC // C_W))
        return _attn(q_arrays, kv_arrays, cm_k, cm_v, bias, mask_a, sink, sel_a, sel_b,
                     sq=ss, n_chunks=1, off_a=ca_k.shape[1] // CHUNK, off_b=(B_KEYS - ss) // CHUNK)

    y_s, kv32_s = trunk(x_sample, pos_s, (n_s // TOKEN_TILE, TOKEN_TILE), attend_sample)
    ka_s, va_s, kb_s, vb_s = split_kv(kv32_s, bs, ss)

    return (y_p, y_s, *prompt_caches, mk_p, mv_p, ka_s[None], va_s[None], kb_s[None], vb_s[None])
```

```python
import functools

import jax
import jax.numpy as jnp
from jax import lax
from jax.experimental import pallas as pl
from jax.experimental.pallas import tpu as pltpu

D_MODEL = 1024
PAST_LEN = 1024
CHUNK = 64
HEAD_DIM = 64
A_Q_HEADS = 8
A_KV_HEADS = 2
A_GROUP = A_Q_HEADS // A_KV_HEADS
A_PREV_CHUNKS = 2
B_HEADS = 4
B_PREV_CHUNKS = 8
REL_CLIP = 128
C_HEADS = 4
N_MEM = 256
FF_DIM = 2816
ROPE_THETA = 10000.0
EPS = 1e-6
NEG = -1e30

A_Q = A_Q_HEADS * HEAD_DIM
A_KV = A_KV_HEADS * HEAD_DIM
B_W = B_HEADS * HEAD_DIM
C_W = C_HEADS * HEAD_DIM
IN_COLS = A_Q + 2 * A_KV + 3 * B_W + C_W
A_KEYS = (A_PREV_CHUNKS + 1) * CHUNK
B_BAND_CHUNKS = B_PREV_CHUNKS + 2
B_KEYS = B_BAND_CHUNKS * CHUNK
OFF_QA, OFF_KA, OFF_VA = 0, A_Q, A_Q + A_KV
OFF_QB, OFF_KB, OFF_VB = A_Q + 2 * A_KV, A_Q + 2 * A_KV + B_W, A_Q + 2 * A_KV + 2 * B_W
OFF_QC = A_Q + 2 * A_KV + 3 * B_W
KV_COLS = 2 * A_KV + 2 * B_W
A_HEAD_ORDER = (0, 4, 1, 5, 2, 6, 3, 7)

V7X_LANES = 128
V7X_MXU_DIM = 256
V7X_VMEM_LIMIT = 56 * 1024 * 1024

TOKEN_TILE = 1024
FF_CHUNKS = (6 * V7X_MXU_DIM, 5 * V7X_MXU_DIM)
assert sum(FF_CHUNKS) == FF_DIM
FFN_SUBTILES = 4
PROJ_SUBTILES = 4
MERGE_SUBTILES = 4
ATTN_CHUNKS_PER_STEP = 32
LOG2E = 1.4426950408889634
BF16 = jnp.bfloat16
F32 = jnp.float32


def _dot(a, b):
    return jnp.dot(a, b, preferred_element_type=F32)


def _dot_nt(a, b):
    return lax.dot_general(a, b, (((1,), (1,)), ((), ())), preferred_element_type=F32)


def _rmsnorm(x, g):
    return x * lax.rsqrt(jnp.mean(x * x, axis=-1, keepdims=True) + EPS) * g


def _sigmoid(z):
    return 1.0 / (1.0 + jnp.exp(-z))


def _const_spec(shape):
    nd = len(shape)
    return pl.BlockSpec(shape, lambda *_: (0,) * nd, pipeline_mode=pl.Buffered(1))


def _params(*sem):
    return pltpu.CompilerParams(dimension_semantics=sem, vmem_limit_bytes=V7X_VMEM_LIMIT)


def _swiglu_residual(x, g_pre, wg_ref, wu_ref, wd_ref):
    hn = _rmsnorm(x, g_pre).astype(BF16)
    acc = None
    lo = 0
    for width in FF_CHUNKS:
        g = _dot(hn, wg_ref[:, lo:lo + width])
        u = _dot(hn, wu_ref[:, lo:lo + width])
        a = (g * _sigmoid(g) * u).astype(BF16)
        d = _dot(a, wd_ref[lo:lo + width, :])
        acc = d if acc is None else acc + d
        lo += width
    return x + 0.5 * acc


def _ffn_kernel(x_ref, gpre_ref, wg_ref, wu_ref, wd_ref, gpost_ref, *out_refs, emit_x):
    n_ref = out_refs[-1]
    sub = TOKEN_TILE // FFN_SUBTILES
    for i in range(FFN_SUBTILES):
        rows = slice(i * sub, (i + 1) * sub)
        y = _swiglu_residual(x_ref[rows, :], gpre_ref[...], wg_ref, wu_ref, wd_ref)
        if emit_x:
            out_refs[0][rows, :] = y
        n_ref[rows, :] = _rmsnorm(y, gpost_ref[...]).astype(n_ref.dtype)


def _ffn(x, g_pre, wg, wu, wd, g_post, *, emit_x, norm_dtype):
    n = x.shape[0]
    assert n % TOKEN_TILE == 0, n
    tile = pl.BlockSpec((TOKEN_TILE, D_MODEL), lambda i: (i, 0))
    out_shape = [jax.ShapeDtypeStruct((n, D_MODEL), norm_dtype)]
    out_specs = [tile]
    if emit_x:
        out_shape.insert(0, jax.ShapeDtypeStruct((n, D_MODEL), F32))
        out_specs.insert(0, tile)
    return pl.pallas_call(
        functools.partial(_ffn_kernel, emit_x=emit_x),
        grid=(n // TOKEN_TILE,),
        in_specs=[tile, _const_spec((1, D_MODEL)), _const_spec(wg.shape), _const_spec(wu.shape),
                  _const_spec(wd.shape), _const_spec((1, D_MODEL))],
        out_specs=out_specs,
        out_shape=out_shape,
        compiler_params=_params("parallel"),
        name="ffn_x" if emit_x else "ffn_final",
    )(x, g_pre, wg, wu, wd, g_post)


def _head_inv_rms(y, bd):
    w = y.shape[1]
    ssq = _dot((y * y).astype(BF16), bd[:w, :w])
    return lax.rsqrt(ssq * (1.0 / HEAD_DIM) + EPS)


def _rope(x, cos, sin_signed, first_half):
    rot = jnp.where(first_half, pltpu.roll(x, V7X_LANES - HEAD_DIM // 2, 1), pltpu.roll(x, HEAD_DIM // 2, 1))
    return x * cos + rot * sin_signed


def _proj_kernel(h_ref, w_ref, gain_ref, cos_ref, sin_ref, bd_ref, q_ref, ka_ref, va_ref, kb_ref, vb_ref, kv32_ref):
    bd = bd_ref[...]
    sub = TOKEN_TILE // PROJ_SUBTILES
    lane = lax.broadcasted_iota(jnp.int32, (sub, V7X_LANES), 1)
    first_half = (lane % HEAD_DIM) < (HEAD_DIM // 2)
    for i in range(PROJ_SUBTILES):
        rows = slice(i * sub, (i + 1) * sub)
        y = _dot(h_ref[0, rows, :], w_ref[...])
        cos = cos_ref[rows, :]
        sin = sin_ref[rows, :]

        def normed(off, width):
            blk = y[:, off:off + width]
            return blk * _head_inv_rms(blk, bd) * gain_ref[:, off:off + width]

        pieces = {}
        for off in range(OFF_QA, OFF_QA + A_Q, V7X_MXU_DIM):
            n = normed(off, V7X_MXU_DIM)
            for s in range(0, V7X_MXU_DIM, V7X_LANES):
                pieces[off + s] = _rope(n[:, s:s + V7X_LANES], cos, sin, first_half)
        pieces[OFF_KA] = _rope(normed(OFF_KA, A_KV), cos, sin, first_half)
        pieces[OFF_VA] = y[:, OFF_VA:OFF_VA + A_KV]
        for off in (OFF_QB, OFF_KB, OFF_QC):
            n = normed(off, V7X_MXU_DIM)
            for s in range(0, V7X_MXU_DIM, V7X_LANES):
                pieces[off + s] = n[:, s:s + V7X_LANES]
        for s in range(0, B_W, V7X_LANES):
            pieces[OFF_VB + s] = y[:, OFF_VB + s:OFF_VB + s + V7X_LANES]

        q_col = 0
        for off, width in ((OFF_QA, A_Q), (OFF_QB, B_W), (OFF_QC, C_W)):
            for s in range(0, width, V7X_LANES):
                q_ref[0, rows, q_col:q_col + V7X_LANES] = pieces[off + s].astype(BF16)
                q_col += V7X_LANES
        col = 0
        for o_ref, off, width in ((ka_ref, OFF_KA, A_KV), (va_ref, OFF_VA, A_KV), (kb_ref, OFF_KB, B_W), (vb_ref, OFF_VB, B_W)):
            for s in range(0, width, V7X_LANES):
                o_ref[0, rows, s:s + V7X_LANES] = pieces[off + s].astype(BF16)
                kv32_ref[0, rows, col:col + V7X_LANES] = pieces[off + s]
                col += V7X_LANES


def _proj(h, w_in, gain, cos, sin, bd):
    b, s, _ = h.shape
    assert s % TOKEN_TILE == 0, s
    widths = (A_Q + B_W + C_W, A_KV, A_KV, B_W, B_W)
    return pl.pallas_call(
        _proj_kernel,
        grid=(b, s // TOKEN_TILE),
        in_specs=[pl.BlockSpec((1, TOKEN_TILE, D_MODEL), lambda i, t: (i, t, 0)),
                  _const_spec(w_in.shape), _const_spec(gain.shape),
                  pl.BlockSpec((TOKEN_TILE, V7X_LANES), lambda i, t: (t, 0)),
                  pl.BlockSpec((TOKEN_TILE, V7X_LANES), lambda i, t: (t, 0)),
                  _const_spec(bd.shape)],
        out_specs=[pl.BlockSpec((1, TOKEN_TILE, w), lambda i, t: (i, t, 0)) for w in widths]
                  + [pl.BlockSpec((1, TOKEN_TILE, KV_COLS), lambda i, t: (i, 0, 0))],
        out_shape=[jax.ShapeDtypeStruct((b, s, w), BF16) for w in widths]
                  + [jax.ShapeDtypeStruct((b, TOKEN_TILE, KV_COLS), F32)],
        compiler_params=_params("parallel", "arbitrary"),
        name="proj",
    )(h, w_in, gain, cos, sin, bd)


def _memkv_kernel(m_ref, gmem_ref, w_ref, gkc_ref, bd_ref, o32_ref, o16_ref):
    hn = _rmsnorm(m_ref[...], gmem_ref[...]).astype(BF16)
    y = _dot(hn, w_ref[...])
    k = y[:, :C_W]
    k = k * _head_inv_rms(k, bd_ref[...]) * gkc_ref[...]
    v = y[:, C_W:]
    o32_ref[:, :C_W] = k
    o32_ref[:, C_W:] = v
    o16_ref[:, :C_W] = k.astype(BF16)
    o16_ref[:, C_W:] = v.astype(BF16)


def _memkv(mem, g_mem, w, g_kc, bd):
    n = mem.shape[0]
    assert n % TOKEN_TILE == 0, n
    return pl.pallas_call(
        _memkv_kernel,
        grid=(n // TOKEN_TILE,),
        in_specs=[pl.BlockSpec((TOKEN_TILE, D_MODEL), lambda i: (i, 0)), _const_spec((1, D_MODEL)),
                  _const_spec(w.shape), _const_spec((1, C_W)), _const_spec(bd.shape)],
        out_specs=[pl.BlockSpec((TOKEN_TILE, 2 * C_W), lambda i: (i, 0))] * 2,
        out_shape=[jax.ShapeDtypeStruct((n, 2 * C_W), F32), jax.ShapeDtypeStruct((n, 2 * C_W), BF16)],
        compiler_params=_params("parallel"),
        name="memkv",
    )(mem, g_mem, w, g_kc, bd)


A_ROWS = A_Q_HEADS * CHUNK
A_SCORE_COLS = 2 * V7X_LANES
B_ROWS = B_HEADS * CHUNK
C_ROWS = C_HEADS * CHUNK


def _attn_kernel(qa_ref, qb_ref, qc_ref, ka_ref, va_ref, kb_ref, vb_ref, mk_ref, mv_ref,
                 bias_ref, maska_ref, sinkpad_ref, sela_ref, selb_ref, y_ref,
                 sa_ref, sb_ref, sc_ref, pa_ref, pb_ref, pc_ref, la_ref, lb_ref, lc_ref,
                 *, n_chunks, off_a, off_b):
    t = pl.program_id(1)
    lane_a = lax.broadcasted_iota(jnp.int32, (CHUNK, V7X_LANES), 1)
    low_half = lane_a < HEAD_DIM
    lane_b = lax.broadcasted_iota(jnp.int32, (CHUNK, B_W), 1) // HEAD_DIM

    def stack_heads(q, sel_ref, n_heads):
        return jnp.concatenate([q * sel_ref[h] for h in range(n_heads)], axis=0)

    def unstack_heads(r, n_heads):
        out = r[(n_heads - 1) * CHUNK:]
        for h in reversed(range(n_heads - 1)):
            out = jnp.where(lane_b == h, r[h * CHUNK:(h + 1) * CHUNK], out)
        return out

    def rows_of(j):
        return pl.ds(pl.multiple_of(j * CHUNK, CHUNK), CHUNK)

    def band(j, off, prev_chunks):
        c = t * n_chunks + j + off
        return pl.multiple_of(jnp.maximum(c - prev_chunks, 0) * CHUNK, CHUNK), jnp.minimum(c, prev_chunks)

    def scores(j, slot, first_chunks):
        rows = rows_of(j)
        start_a, var_a = band(j, off_a, A_PREV_CHUNKS)
        qa = qa_ref[0, rows, :]
        lhs = jnp.concatenate(
            [qa[:, (p // 2) * V7X_LANES:(p // 2 + 1) * V7X_LANES] * sela_ref[p % 2] for p in range(A_Q_HEADS)],
            axis=0)
        s = _dot_nt(lhs, ka_ref[0, pl.ds(start_a, A_KEYS), :])
        if first_chunks:
            s = s + maska_ref[var_a][0:1]
        sa_ref[slot, :, :A_KEYS] = s
        start_b, var_b = band(j, off_b, B_BAND_CHUNKS - 1)
        lhs = stack_heads(qb_ref[0, rows, :], selb_ref, B_HEADS)
        sb_ref[slot] = _dot_nt(lhs, kb_ref[0, pl.ds(start_b, B_KEYS), :]) + bias_ref[var_b]
        lhs = stack_heads(qc_ref[0, rows, :], selb_ref, C_HEADS)
        sc_ref[slot] = _dot_nt(lhs, mk_ref[0])

    def numerators(slot):
        for s_ref, p_ref, l_ref in ((sa_ref, pa_ref, la_ref), (sb_ref, pb_ref, lb_ref), (sc_ref, pc_ref, lc_ref)):
            s = s_ref[slot]
            e = jnp.exp2(s - jnp.max(s, axis=-1, keepdims=True))
            p_ref[slot] = e.astype(BF16)
            l_ref[slot] = jnp.broadcast_to(1.0 / jnp.sum(e, axis=-1, keepdims=True), l_ref.shape[1:])

    def outputs(j, slot):
        rows = rows_of(j)
        start_a, _ = band(j, off_a, A_PREV_CHUNKS)
        r = _dot(pa_ref[slot, :, :A_KEYS], va_ref[0, pl.ds(start_a, A_KEYS), :]) * la_ref[slot]
        for jc in range(A_Q // V7X_LANES):
            ev = r[(2 * jc) * CHUNK:(2 * jc + 1) * CHUNK]
            od = r[(2 * jc + 1) * CHUNK:(2 * jc + 2) * CHUNK]
            y_ref[0, rows, jc * V7X_LANES:(jc + 1) * V7X_LANES] = jnp.where(low_half, ev, od).astype(BF16)
        start_b, _ = band(j, off_b, B_BAND_CHUNKS - 1)
        inv_l = lb_ref[slot]
        r = _dot(pb_ref[slot], vb_ref[0, pl.ds(start_b, B_KEYS), :]) * jnp.concatenate([inv_l, inv_l], axis=1)
        y_ref[0, rows, A_Q:A_Q + B_W] = unstack_heads(r, B_HEADS).astype(BF16)
        inv_l = lc_ref[slot]
        r = _dot(pc_ref[slot], mv_ref[0]) * jnp.concatenate([inv_l, inv_l], axis=1)
        y_ref[0, rows, A_Q + B_W:] = unstack_heads(r, C_HEADS).astype(BF16)

    for slot in range(2):
        sa_ref[slot, :, A_KEYS:] = sinkpad_ref[...]

    if n_chunks < 4:
        for j in range(n_chunks):
            scores(j, 0, j + off_a < A_PREV_CHUNKS)
            numerators(0)
            outputs(j, 0)
        return

    assert n_chunks % 2 == 0 and A_PREV_CHUNKS <= 2
    scores(0, 0, True)
    scores(1, 1, True)
    numerators(0)

    def pair(i, carry):
        j = 2 * i
        scores(j, 0, False)
        numerators(1)
        outputs(j - 2, 0)
        scores(j + 1, 1, False)
        numerators(0)
        outputs(j - 1, 1)
        return carry

    lax.fori_loop(1, n_chunks // 2, pair, 0)
    numerators(1)
    outputs(n_chunks - 2, 0)
    outputs(n_chunks - 1, 1)


def _attn(q_arrays, kv_arrays, mk, mv, bias, mask_a, sink, sel_a, sel_b, *, sq, n_chunks, off_a, off_b):
    b = q_arrays[0][0].shape[0]
    tq = n_chunks * CHUNK
    q_widths = (A_Q, B_W, C_W)
    kv_widths = (A_KV, A_KV, B_W, B_W)
    in_specs, args = [], []
    for (arr, cb), w in zip(q_arrays, q_widths):
        in_specs.append(pl.BlockSpec((1, tq, w), lambda i, t, cb=cb: (i, t, cb)))
        args.append(arr)
    for (arr, cb), w in zip(kv_arrays, kv_widths):
        in_specs.append(pl.BlockSpec((1, arr.shape[1], w), lambda i, t, cb=cb: (i, 0, cb)))
        args.append(arr)
    for arr in (mk, mv):
        in_specs.append(pl.BlockSpec((1, N_MEM, C_W), lambda i, t: (i, 0, 0)))
        args.append(arr)
    for arr in (bias, mask_a, sink, sel_a, sel_b):
        in_specs.append(_const_spec(arr.shape))
        args.append(arr)
    return pl.pallas_call(
        functools.partial(_attn_kernel, n_chunks=n_chunks, off_a=off_a, off_b=off_b),
        grid=(b, sq // tq),
        in_specs=in_specs,
        out_specs=pl.BlockSpec((1, tq, D_MODEL), lambda i, t: (i, t, 0)),
        out_shape=jax.ShapeDtypeStruct((b, sq, D_MODEL), BF16),
        scratch_shapes=[pltpu.VMEM((2, A_ROWS, A_SCORE_COLS), F32), pltpu.VMEM((2, B_ROWS, B_KEYS), F32),
                        pltpu.VMEM((2, C_ROWS, N_MEM), F32),
                        pltpu.VMEM((2, A_ROWS, A_SCORE_COLS), BF16), pltpu.VMEM((2, B_ROWS, B_KEYS), BF16),
                        pltpu.VMEM((2, C_ROWS, N_MEM), BF16),
                        pltpu.VMEM((2, A_ROWS, V7X_LANES), F32), pltpu.VMEM((2, B_ROWS, V7X_LANES), F32),
                        pltpu.VMEM((2, C_ROWS, V7X_LANES), F32)],
        compiler_params=_params("parallel", "arbitrary"),
        name="attn",
    )(*args)


def _merge_kernel(h_ref, y_ref, x_ref, wgate_ref, bgate_ref, wa_ref, wb_ref, wc_ref, wout_ref, o_ref):
    sub = TOKEN_TILE // MERGE_SUBTILES
    for t in range(MERGE_SUBTILES):
        rows = slice(t * sub, (t + 1) * sub)
        h = h_ref[rows, :]
        merged = None
        col = 0
        for i, (w_ref, width) in enumerate(((wa_ref, A_Q), (wb_ref, B_W), (wc_ref, C_W))):
            gate = _sigmoid(_dot(h, wgate_ref[:, i * D_MODEL:(i + 1) * D_MODEL])
                            + bgate_ref[:, i * D_MODEL:(i + 1) * D_MODEL])
            term = gate * _dot(y_ref[rows, col:col + width], w_ref[...])
            merged = term if merged is None else merged + term
            col += width
        o_ref[rows, :] = x_ref[rows, :] + _dot(merged.astype(BF16), wout_ref[...])


def _merge(h, y, x, w_gate, b_gate, wa, wb, wc, w_out):
    n = h.shape[0]
    assert n % TOKEN_TILE == 0, n
    tile = pl.BlockSpec((TOKEN_TILE, D_MODEL), lambda i: (i, 0))
    consts = (w_gate, b_gate, wa, wb, wc, w_out)
    return pl.pallas_call(
        _merge_kernel,
        grid=(n // TOKEN_TILE,),
        in_specs=[tile, tile, tile] + [_const_spec(c.shape) for c in consts],
        out_specs=tile,
        out_shape=jax.ShapeDtypeStruct((n, D_MODEL), F32),
        compiler_params=_params("parallel"),
        name="merge",
    )(h, y, x, *consts)


def _rope_tables(pos):
    half = HEAD_DIM // 2
    inv = ROPE_THETA ** (-jnp.arange(half, dtype=F32) / half)
    ang = pos.astype(F32)[:, None] * inv[None, :]
    cos, sin = jnp.cos(ang), jnp.sin(ang)
    reps = V7X_LANES // HEAD_DIM
    return jnp.tile(jnp.concatenate([cos, cos], -1), (1, reps)), jnp.tile(jnp.concatenate([-sin, sin], -1), (1, reps))


def _bias_tables(rel_bias):
    rel_bias = rel_bias.astype(F32)
    near_lags = (REL_CLIP + CHUNK - 1) // CHUNK + 1
    ext = jnp.pad(rel_bias, ((0, 0), (0, near_lags * CHUNK - REL_CLIP)), mode="edge")
    period = 2 * CHUNK
    table = jnp.broadcast_to(rel_bias[:, None, -1:], (B_HEADS, CHUNK, B_KEYS))
    lag = jnp.arange(B_BAND_CHUNKS)[:, None, None, None] - (jnp.arange(B_KEYS) // CHUNK)[None, None, None, :]
    table = jnp.broadcast_to(table[None], (B_BAND_CHUNKS,) + table.shape)
    for g in range(near_lags):
        seg = ext[:, g * CHUNK + REL_CLIP - CHUNK + 1:g * CHUNK + REL_CLIP + CHUNK]
        ring = jnp.concatenate([seg[:, CHUNK - 1::-1], jnp.zeros((B_HEADS, 1), F32), seg[:, :CHUNK - 1:-1]], axis=1)
        block = jnp.tile(ring, (1, CHUNK))[:, :CHUNK * (period - 1)].reshape(B_HEADS, CHUNK, period - 1)[:, :, :CHUNK]
        table = jnp.where(lag == g, jnp.tile(block, (1, 1, B_BAND_CHUNKS))[None], table)
    visible = (lag >= 0) & (lag <= B_PREV_CHUNKS)
    return jnp.where(visible, table, NEG).reshape(B_BAND_CHUNKS, B_HEADS * CHUNK, B_KEYS)


def _mask_a_tables():
    j = jnp.arange(A_KEYS)[None, :]
    rows = [jnp.where((j // CHUNK) <= v, 0.0, NEG) for v in range(A_PREV_CHUNKS + 1)]
    return jnp.stack([jnp.broadcast_to(r, (8, A_KEYS)) for r in rows]).astype(F32)


def _block_diag_ones():
    i = jnp.arange(V7X_MXU_DIM)
    return (i[:, None] // HEAD_DIM == i[None, :] // HEAD_DIM).astype(BF16)


def _lane_selectors():
    lane = jnp.arange(V7X_LANES)[None, :] // HEAD_DIM
    sel_a = jnp.stack([jnp.broadcast_to(lane == p, (CHUNK, V7X_LANES)) for p in range(2)]).astype(BF16)
    lane = jnp.arange(B_W)[None, :] // HEAD_DIM
    sel_b = jnp.stack([jnp.broadcast_to(lane == h, (CHUNK, B_W)) for h in range(B_HEADS)]).astype(BF16)
    return sel_a, sel_b


def kernel(x_prompt, x_sample, cache_a_k, cache_a_v, cache_b_k, cache_b_v, cache_mem_k, cache_mem_v,
           mem_prompt, g_ff1, w_ff1_gate, w_ff1_up, w_ff1_down, g_mix, w_in, g_qa, g_ka, sinks_a,
           g_qb, g_kb, rel_bias_b, g_qc, g_mem, w_mem_kv, g_kc, w_gate, b_gate, w_br_a, w_br_b,
           w_br_c, w_out, g_ff2, w_ff2_gate, w_ff2_up, w_ff2_down, g_final):
    bp, sp, _ = x_prompt.shape
    bs, ss, _ = x_sample.shape
    l = 0
    row = lambda g: g[l].reshape(1, -1).astype(F32)
    bf = lambda w: w[l].astype(BF16)

    order = jnp.array(A_HEAD_ORDER)
    head_cols = (order[:, None] * HEAD_DIM + jnp.arange(HEAD_DIM)[None, :]).reshape(-1)
    w_in_l = w_in[l]
    w_in_bf = jnp.concatenate([w_in_l[:, :A_Q][:, head_cols], w_in_l[:, A_Q:]], axis=1).astype(BF16)
    w_br_a_bf = w_br_a[l][head_cols].astype(BF16)
    scale = HEAD_DIM ** -0.5 * LOG2E
    ones = lambda n: jnp.ones((n,), F32)
    gain = jnp.concatenate([
        jnp.tile(g_qa[l], A_Q_HEADS) * scale, jnp.tile(g_ka[l], A_KV_HEADS), ones(A_KV),
        jnp.tile(g_qb[l], B_HEADS) * scale, jnp.tile(g_kb[l], B_HEADS), ones(B_W),
        jnp.tile(g_qc[l], C_HEADS) * scale]).reshape(1, IN_COLS).astype(F32)
    sink = jnp.concatenate([jnp.repeat(sinks_a[l][order].astype(F32) * LOG2E, CHUNK)[:, None],
                            jnp.full((A_ROWS, A_SCORE_COLS - A_KEYS - 1), NEG, F32)], axis=1)
    bias = _bias_tables(rel_bias_b[l] * LOG2E)
    mask_a = _mask_a_tables()
    bd = _block_diag_ones()
    sel_a, sel_b = _lane_selectors()
    ff1 = (row(g_ff1), bf(w_ff1_gate), bf(w_ff1_up), bf(w_ff1_down), row(g_mix))
    ff2 = (row(g_ff2), bf(w_ff2_gate), bf(w_ff2_up), bf(w_ff2_down), row(g_final))
    merge_w = (bf(w_gate), row(b_gate), w_br_a_bf, bf(w_br_b), bf(w_br_c), bf(w_out))

    def trunk(x, pos, batch_view, attend):
        n = x.shape[0] * x.shape[1]
        xf = x.reshape(n, D_MODEL)
        x1, h = _ffn(xf, *ff1, emit_x=True, norm_dtype=BF16)
        cos, sin = _rope_tables(pos)
        *qkv, kv32 = _proj(h.reshape(batch_view + (D_MODEL,)), w_in_bf, gain, cos, sin, bd)
        y = attend(*qkv)
        x2 = _merge(h, y.reshape(n, D_MODEL), x1, *merge_w)
        (out,) = _ffn(x2, *ff2, emit_x=False, norm_dtype=F32)
        return out.reshape(x.shape), kv32

    def split_kv(kv32, b, rows):
        kv32 = kv32.reshape(b, rows, KV_COLS)
        ka = kv32[..., :A_KV].reshape(b, rows, A_KV_HEADS, HEAD_DIM)
        va = kv32[..., A_KV:2 * A_KV].reshape(b, rows, A_KV_HEADS, HEAD_DIM)
        kb = kv32[..., 2 * A_KV:2 * A_KV + B_W].reshape(b, rows, B_HEADS, HEAD_DIM)
        vb = kv32[..., 2 * A_KV + B_W:].reshape(b, rows, B_HEADS, HEAD_DIM)
        return ka, va, kb, vb

    g_kc_row = jnp.tile(g_kc[l], C_HEADS).reshape(1, C_W).astype(F32)
    mem32, mem16 = _memkv(mem_prompt.reshape(bp * N_MEM, D_MODEL), row(g_mem), bf(w_mem_kv), g_kc_row, bd)
    mem16 = mem16.reshape(bp, N_MEM, 2 * C_W)
    mk_p = mem32[:, :C_W].reshape(1, bp, N_MEM, C_HEADS, HEAD_DIM)
    mv_p = mem32[:, C_W:].reshape(1, bp, N_MEM, C_HEADS, HEAD_DIM)

    q_blocks = (0, A_Q // B_W, (A_Q + B_W) // C_W)

    def attend_prompt(q, ka16, va16, kb16, vb16):
        q_arrays = tuple((q, cb) for cb in q_blocks)
        kv_arrays = ((ka16, 0), (va16, 0), (kb16, 0), (vb16, 0))
        return _attn(q_arrays, kv_arrays, mem16[..., :C_W], mem16[..., C_W:], bias, mask_a, sink, sel_a, sel_b,
                     sq=sp, n_chunks=min(ATTN_CHUNKS_PER_STEP, sp // CHUNK), off_a=0, off_b=0)

    y_p, kv32_p = trunk(x_prompt, jnp.arange(sp, dtype=jnp.int32), (bp, sp), attend_prompt)
    ka, va, kb, vb = split_kv(kv32_p, bp, TOKEN_TILE)
    keep_a = min(A_PREV_CHUNKS * CHUNK, sp)
    keep_b = min(B_PREV_CHUNKS * CHUNK, sp)
    prompt_caches = (ka[:, TOKEN_TILE - keep_a:][None], va[:, TOKEN_TILE - keep_a:][None],
                     kb[:, TOKEN_TILE - keep_b:][None], vb[:, TOKEN_TILE - keep_b:][None])

    n_s = bs * ss
    rows_per_tile = TOKEN_TILE // ss
    pos_s = PAST_LEN + jnp.tile(jnp.arange(ss, dtype=jnp.int32), rows_per_tile)
    flat16 = lambda c: c[l].reshape(c.shape[1], c.shape[2], -1).astype(BF16)
    ca_k, ca_v, cb_k, cb_v = flat16(cache_a_k), flat16(cache_a_v), flat16(cache_b_k), flat16(cache_b_v)
    cm_k, cm_v = flat16(cache_mem_k), flat16(cache_mem_v)

    def attend_sample(q, ka16, va16, kb16, vb16):
        per_seq = lambda a: a.reshape(bs, ss, a.shape[-1])
        lead = jnp.zeros((bs, B_KEYS - cb_k.shape[1] - ss, B_W), BF16)
        kv_arrays = ((jnp.concatenate([ca_k, per_seq(ka16)], 1), 0), (jnp.concatenate([ca_v, per_seq(va16)], 1), 0),
                     (jnp.concatenate([lead, cb_k, per_seq(kb16)], 1), 0),
                     (jnp.concatenate([lead, cb_v, per_seq(vb16)], 1), 0))
        q_arrays = tuple((per_seq(q), cb) for cb in q_blocks)
        return _attn(q_arrays, kv_arrays, cm_k, cm_v, bias, mask_a, sink, sel_a, sel_b,
                     sq=ss, n_chunks=1, off_a=ca_k.shape[1] // CHUNK, off_b=(B_KEYS - ss) // CHUNK)

    y_s, kv32_s = trunk(x_sample, pos_s, (n_s // TOKEN_TILE, TOKEN_TILE), attend_sample)
    ka_s, va_s, kb_s, vb_s = split_kv(kv32_s, bs, ss)

    return (y_p, y_s, *prompt_caches, mk_p, mv_p, ka_s[None], va_s[None], kb_s[None], vb_s[None])
```

```python
import functools

import jax
import jax.numpy as jnp
from jax import lax
from jax.experimental import pallas as pl
from jax.experimental.pallas import tpu as pltpu

D_MODEL = 1024
PAST_LEN = 1024
CHUNK = 64
HEAD_DIM = 64
A_Q_HEADS = 8
A_KV_HEADS = 2
A_GROUP = A_Q_HEADS // A_KV_HEADS
A_PREV_CHUNKS = 2
B_HEADS = 4
B_PREV_CHUNKS = 8
REL_CLIP = 128
C_HEADS = 4
N_MEM = 256
FF_DIM = 2816
ROPE_THETA = 10000.0
EPS = 1e-6
NEG = -1e30

A_Q = A_Q_HEADS * HEAD_DIM
A_KV = A_KV_HEADS * HEAD_DIM
B_W = B_HEADS * HEAD_DIM
C_W = C_HEADS * HEAD_DIM
IN_COLS = A_Q + 2 * A_KV + 3 * B_W + C_W
A_KEYS = (A_PREV_CHUNKS + 1) * CHUNK
B_BAND_CHUNKS = B_PREV_CHUNKS + 2
B_KEYS = B_BAND_CHUNKS * CHUNK
OFF_QA, OFF_KA, OFF_VA = 0, A_Q, A_Q + A_KV
OFF_QB, OFF_KB, OFF_VB = A_Q + 2 * A_KV, A_Q + 2 * A_KV + B_W, A_Q + 2 * A_KV + 2 * B_W
OFF_QC = A_Q + 2 * A_KV + 3 * B_W
KV_COLS = 2 * A_KV + 2 * B_W
A_HEAD_ORDER = (0, 4, 1, 5, 2, 6, 3, 7)

V7X_LANES = 128
V7X_MXU_DIM = 256
V7X_VMEM_LIMIT = 56 * 1024 * 1024

TOKEN_TILE = 1024
FF_CHUNKS = (6 * V7X_MXU_DIM, 5 * V7X_MXU_DIM)
assert sum(FF_CHUNKS) == FF_DIM
FFN_SUBTILES = 4
PROJ_SUBTILES = 4
MERGE_SUBTILES = 4
ATTN_CHUNKS_PER_STEP = 32
ATTN_SLOTS = 4
LOG2E = 1.4426950408889634
BF16 = jnp.bfloat16
F32 = jnp.float32


def _dot(a, b):
    return jnp.dot(a, b, preferred_element_type=F32)


def _dot_nt(a, b):
    return lax.dot_general(a, b, (((1,), (1,)), ((), ())), preferred_element_type=F32)


def _rmsnorm(x, g):
    return x * lax.rsqrt(jnp.mean(x * x, axis=-1, keepdims=True) + EPS) * g


def _sigmoid(z):
    return 1.0 / (1.0 + jnp.exp(-z))


def _const_spec(shape):
    nd = len(shape)
    return pl.BlockSpec(shape, lambda *_: (0,) * nd, pipeline_mode=pl.Buffered(1))


def _params(*sem):
    return pltpu.CompilerParams(dimension_semantics=sem, vmem_limit_bytes=V7X_VMEM_LIMIT)


def _swiglu_residual(x, g_pre, wg_ref, wu_ref, wd_ref):
    hn = _rmsnorm(x, g_pre).astype(BF16)
    acc = None
    lo = 0
    for width in FF_CHUNKS:
        g = _dot(hn, wg_ref[:, lo:lo + width])
        u = _dot(hn, wu_ref[:, lo:lo + width])
        a = (g * _sigmoid(g) * u).astype(BF16)
        d = _dot(a, wd_ref[lo:lo + width, :])
        acc = d if acc is None else acc + d
        lo += width
    return x + 0.5 * acc


def _ffn_kernel(x_ref, gpre_ref, wg_ref, wu_ref, wd_ref, gpost_ref, *out_refs, emit_x):
    n_ref = out_refs[-1]
    sub = TOKEN_TILE // FFN_SUBTILES
    for i in range(FFN_SUBTILES):
        rows = slice(i * sub, (i + 1) * sub)
        y = _swiglu_residual(x_ref[rows, :], gpre_ref[...], wg_ref, wu_ref, wd_ref)
        if emit_x:
            out_refs[0][rows, :] = y
        n_ref[rows, :] = _rmsnorm(y, gpost_ref[...]).astype(n_ref.dtype)


def _ffn(x, g_pre, wg, wu, wd, g_post, *, emit_x, norm_dtype):
    n = x.shape[0]
    assert n % TOKEN_TILE == 0, n
    tile = pl.BlockSpec((TOKEN_TILE, D_MODEL), lambda i: (i, 0))
    out_shape = [jax.ShapeDtypeStruct((n, D_MODEL), norm_dtype)]
    out_specs = [tile]
    if emit_x:
        out_shape.insert(0, jax.ShapeDtypeStruct((n, D_MODEL), F32))
        out_specs.insert(0, tile)
    return pl.pallas_call(
        functools.partial(_ffn_kernel, emit_x=emit_x),
        grid=(n // TOKEN_TILE,),
        in_specs=[tile, _const_spec((1, D_MODEL)), _const_spec(wg.shape), _const_spec(wu.shape),
                  _const_spec(wd.shape), _const_spec((1, D_MODEL))],
        out_specs=out_specs,
        out_shape=out_shape,
        compiler_params=_params("parallel"),
        name="ffn_x" if emit_x else "ffn_final",
    )(x, g_pre, wg, wu, wd, g_post)


def _head_inv_rms(y, bd):
    w = y.shape[1]
    ssq = _dot((y * y).astype(BF16), bd[:w, :w])
    return lax.rsqrt(ssq * (1.0 / HEAD_DIM) + EPS)


def _rope(x, cos, sin_signed, first_half):
    rot = jnp.where(first_half, pltpu.roll(x, V7X_LANES - HEAD_DIM // 2, 1), pltpu.roll(x, HEAD_DIM // 2, 1))
    return x * cos + rot * sin_signed


def _proj_kernel(h_ref, w_ref, gain_ref, cos_ref, sin_ref, bd_ref, q_ref, ka_ref, va_ref, kb_ref, vb_ref, *f32_refs,
                 n_tiles, keep):
    bd = bd_ref[...]
    stash_rows = 0 if keep is None else max(keep)
    stash_ref = None if keep is None else f32_refs[-1]
    sub = TOKEN_TILE // PROJ_SUBTILES
    lane = lax.broadcasted_iota(jnp.int32, (sub, V7X_LANES), 1)
    first_half = (lane % HEAD_DIM) < (HEAD_DIM // 2)
    for i in range(PROJ_SUBTILES):
        rows = slice(i * sub, (i + 1) * sub)
        y = _dot(h_ref[0, rows, :], w_ref[...])
        cos = cos_ref[rows, :]
        sin = sin_ref[rows, :]

        def normed(off, width):
            blk = y[:, off:off + width]
            return blk * _head_inv_rms(blk, bd) * gain_ref[:, off:off + width]

        pieces = {}
        for off in range(OFF_QA, OFF_QA + A_Q, V7X_MXU_DIM):
            n = normed(off, V7X_MXU_DIM)
            for s in range(0, V7X_MXU_DIM, V7X_LANES):
                pieces[off + s] = _rope(n[:, s:s + V7X_LANES], cos, sin, first_half)
        pieces[OFF_KA] = _rope(normed(OFF_KA, A_KV), cos, sin, first_half)
        pieces[OFF_VA] = y[:, OFF_VA:OFF_VA + A_KV]
        for off in (OFF_QB, OFF_KB, OFF_QC):
            n = normed(off, V7X_MXU_DIM)
            for s in range(0, V7X_MXU_DIM, V7X_LANES):
                pieces[off + s] = n[:, s:s + V7X_LANES]
        for s in range(0, B_W, V7X_LANES):
            pieces[OFF_VB + s] = y[:, OFF_VB + s:OFF_VB + s + V7X_LANES]

        q_col = 0
        for off, width in ((OFF_QA, A_Q), (OFF_QB, B_W), (OFF_QC, C_W)):
            for s in range(0, width, V7X_LANES):
                q_ref[0, rows, q_col:q_col + V7X_LANES] = pieces[off + s].astype(BF16)
                q_col += V7X_LANES
        tail_lo = max(i * sub, TOKEN_TILE - stash_rows)
        col = 0
        for o_ref, off, width in ((ka_ref, OFF_KA, A_KV), (va_ref, OFF_VA, A_KV), (kb_ref, OFF_KB, B_W), (vb_ref, OFF_VB, B_W)):
            for s in range(0, width, V7X_LANES):
                o_ref[0, rows, s:s + V7X_LANES] = pieces[off + s].astype(BF16)
                if keep is None:
                    f32_refs[0][0, rows, col:col + V7X_LANES] = pieces[off + s]
                elif tail_lo < (i + 1) * sub:
                    stash_ref[tail_lo - (TOKEN_TILE - stash_rows):(i + 1) * sub - (TOKEN_TILE - stash_rows),
                              col:col + V7X_LANES] = pieces[off + s][tail_lo - i * sub:]
                col += V7X_LANES

    if keep is None:
        return

    def write_tails():
        col = 0
        for o_ref, width, n_rows in zip(f32_refs[:4], (A_KV, A_KV, B_W, B_W), (keep[0], keep[0], keep[1], keep[1])):
            for s in range(0, width, V7X_LANES):
                o_ref[0, s:s + V7X_LANES, :] = stash_ref[stash_rows - n_rows:, col:col + V7X_LANES].T
                col += V7X_LANES

    if n_tiles == 1:
        write_tails()
    else:
        pl.when(pl.program_id(1) == n_tiles - 1)(write_tails)


def _proj(h, w_in, gain, cos, sin, bd, keep=None):
    b, s, _ = h.shape
    assert s % TOKEN_TILE == 0, s
    widths = (A_Q + B_W + C_W, A_KV, A_KV, B_W, B_W)
    out_specs = [pl.BlockSpec((1, TOKEN_TILE, w), lambda i, t: (i, t, 0)) for w in widths]
    out_shape = [jax.ShapeDtypeStruct((b, s, w), BF16) for w in widths]
    scratch = []
    if keep is None:
        out_specs.append(pl.BlockSpec((1, TOKEN_TILE, KV_COLS), lambda i, t: (i, 0, 0)))
        out_shape.append(jax.ShapeDtypeStruct((b, TOKEN_TILE, KV_COLS), F32))
    else:
        assert max(keep) <= TOKEN_TILE
        for w, n_rows in zip((A_KV, A_KV, B_W, B_W), (keep[0], keep[0], keep[1], keep[1])):
            out_specs.append(pl.BlockSpec((1, w, n_rows), lambda i, t: (i, 0, 0)))
            out_shape.append(jax.ShapeDtypeStruct((b, w, n_rows), F32))
        scratch.append(pltpu.VMEM((max(keep), KV_COLS), F32))
    return pl.pallas_call(
        functools.partial(_proj_kernel, n_tiles=s // TOKEN_TILE, keep=keep),
        grid=(b, s // TOKEN_TILE),
        in_specs=[pl.BlockSpec((1, TOKEN_TILE, D_MODEL), lambda i, t: (i, t, 0)),
                  _const_spec(w_in.shape), _const_spec(gain.shape),
                  pl.BlockSpec((TOKEN_TILE, V7X_LANES), lambda i, t: (t, 0)),
                  pl.BlockSpec((TOKEN_TILE, V7X_LANES), lambda i, t: (t, 0)),
                  _const_spec(bd.shape)],
        out_specs=out_specs,
        out_shape=out_shape,
        scratch_shapes=scratch,
        compiler_params=_params("parallel", "arbitrary"),
        name="proj",
    )(h, w_in, gain, cos, sin, bd)


def _memkv_kernel(m_ref, gmem_ref, w_ref, gkc_ref, bd_ref, k32t_ref, v32t_ref, o16_ref):
    hn = _rmsnorm(m_ref[...], gmem_ref[...]).astype(BF16)
    y = _dot(hn, w_ref[...])
    k = y[:, :C_W]
    k = k * _head_inv_rms(k, bd_ref[...]) * gkc_ref[...]
    v = y[:, C_W:]
    o16_ref[:, :C_W] = k.astype(BF16)
    o16_ref[:, C_W:] = v.astype(BF16)
    for e in range(TOKEN_TILE // N_MEM):
        tok = slice(e * N_MEM, (e + 1) * N_MEM)
        k32t_ref[e] = k[tok].T
        v32t_ref[e] = v[tok].T


def _memkv(mem, g_mem, w, g_kc, bd):
    n = mem.shape[0]
    assert n % TOKEN_TILE == 0 and TOKEN_TILE % N_MEM == 0, n
    seqs = TOKEN_TILE // N_MEM
    return pl.pallas_call(
        _memkv_kernel,
        grid=(n // TOKEN_TILE,),
        in_specs=[pl.BlockSpec((TOKEN_TILE, D_MODEL), lambda i: (i, 0)), _const_spec((1, D_MODEL)),
                  _const_spec(w.shape), _const_spec((1, C_W)), _const_spec(bd.shape)],
        out_specs=[pl.BlockSpec((seqs, C_W, N_MEM), lambda i: (i, 0, 0))] * 2
                  + [pl.BlockSpec((TOKEN_TILE, 2 * C_W), lambda i: (i, 0))],
        out_shape=[jax.ShapeDtypeStruct((n // N_MEM, C_W, N_MEM), F32)] * 2
                  + [jax.ShapeDtypeStruct((n, 2 * C_W), BF16)],
        compiler_params=_params("parallel"),
        name="memkv",
    )(mem, g_mem, w, g_kc, bd)


A_ROWS = A_Q_HEADS * CHUNK
A_SCORE_COLS = 2 * V7X_LANES
B_ROWS = B_HEADS * CHUNK
C_ROWS = C_HEADS * CHUNK


def _attn_kernel(qa_ref, qb_ref, qc_ref, ka_ref, va_ref, kb_ref, vb_ref, mk_ref, mv_ref,
                 bias_ref, maska_ref, sinkpad_ref, sela_ref, selb_ref, y_ref,
                 sa_ref, sb_ref, sc_ref, pa_ref, pb_ref, pc_ref, la_ref, lb_ref, lc_ref,
                 *, n_chunks, off_a, off_b):
    t = pl.program_id(1)
    lane_a = lax.broadcasted_iota(jnp.int32, (CHUNK, V7X_LANES), 1)
    low_half = lane_a < HEAD_DIM
    lane_b = lax.broadcasted_iota(jnp.int32, (CHUNK, B_W), 1) // HEAD_DIM

    def stack_heads(q, sel_ref, n_heads):
        return jnp.concatenate([q * sel_ref[h] for h in range(n_heads)], axis=0)

    def unstack_heads(r, n_heads):
        out = r[(n_heads - 1) * CHUNK:]
        for h in reversed(range(n_heads - 1)):
            out = jnp.where(lane_b == h, r[h * CHUNK:(h + 1) * CHUNK], out)
        return out

    def rows_of(j):
        return pl.ds(pl.multiple_of(j * CHUNK, CHUNK), CHUNK)

    def band(j, off, prev_chunks):
        c = t * n_chunks + j + off
        return pl.multiple_of(jnp.maximum(c - prev_chunks, 0) * CHUNK, CHUNK), jnp.minimum(c, prev_chunks)

    def scores(j, slot, first_chunks):
        rows = rows_of(j)
        start_a, var_a = band(j, off_a, A_PREV_CHUNKS)
        qa = qa_ref[0, rows, :]
        lhs = jnp.concatenate(
            [qa[:, (p // 2) * V7X_LANES:(p // 2 + 1) * V7X_LANES] * sela_ref[p % 2] for p in range(A_Q_HEADS)],
            axis=0)
        s = _dot_nt(lhs, ka_ref[0, pl.ds(start_a, A_KEYS), :])
        if first_chunks:
            s = s + maska_ref[var_a][0:1]
        sa_ref[slot, :, :A_KEYS] = s
        start_b, var_b = band(j, off_b, B_BAND_CHUNKS - 1)
        lhs = stack_heads(qb_ref[0, rows, :], selb_ref, B_HEADS)
        sb_ref[slot] = _dot_nt(lhs, kb_ref[0, pl.ds(start_b, B_KEYS), :]) + bias_ref[var_b]
        lhs = stack_heads(qc_ref[0, rows, :], selb_ref, C_HEADS)
        sc_ref[slot] = _dot_nt(lhs, mk_ref[0])

    def numerators(slot):
        for s_ref, p_ref, l_ref in ((sa_ref, pa_ref, la_ref), (sb_ref, pb_ref, lb_ref), (sc_ref, pc_ref, lc_ref)):
            s = s_ref[slot]
            e = jnp.exp2(s - jnp.max(s, axis=-1, keepdims=True))
            p_ref[slot] = e.astype(BF16)
            l_ref[slot] = jnp.broadcast_to(1.0 / jnp.sum(e, axis=-1, keepdims=True), l_ref.shape[1:])

    def outputs(j, slot):
        rows = rows_of(j)
        start_a, _ = band(j, off_a, A_PREV_CHUNKS)
        r = _dot(pa_ref[slot, :, :A_KEYS], va_ref[0, pl.ds(start_a, A_KEYS), :]) * la_ref[slot]
        for jc in range(A_Q // V7X_LANES):
            ev = r[(2 * jc) * CHUNK:(2 * jc + 1) * CHUNK]
            od = r[(2 * jc + 1) * CHUNK:(2 * jc + 2) * CHUNK]
            y_ref[0, rows, jc * V7X_LANES:(jc + 1) * V7X_LANES] = jnp.where(low_half, ev, od).astype(BF16)
        start_b, _ = band(j, off_b, B_BAND_CHUNKS - 1)
        inv_l = lb_ref[slot]
        r = _dot(pb_ref[slot], vb_ref[0, pl.ds(start_b, B_KEYS), :]) * jnp.concatenate([inv_l, inv_l], axis=1)
        y_ref[0, rows, A_Q:A_Q + B_W] = unstack_heads(r, B_HEADS).astype(BF16)
        inv_l = lc_ref[slot]
        r = _dot(pc_ref[slot], mv_ref[0]) * jnp.concatenate([inv_l, inv_l], axis=1)
        y_ref[0, rows, A_Q + B_W:] = unstack_heads(r, C_HEADS).astype(BF16)

    for slot in range(ATTN_SLOTS):
        sa_ref[slot, :, A_KEYS:] = sinkpad_ref[...]

    if n_chunks < ATTN_SLOTS:
        for j in range(n_chunks):
            scores(j, 0, j + off_a < A_PREV_CHUNKS)
            numerators(0)
            outputs(j, 0)
        return

    assert A_PREV_CHUNKS <= 2

    def step(j, slot):
        scores(j, slot, False)
        numerators((slot - 1) % ATTN_SLOTS)
        outputs(j - 2, (slot - 2) % ATTN_SLOTS)

    scores(0, 0, True)
    scores(1, 1, True)
    numerators(0)
    first = 2
    n_groups = (n_chunks - first) // ATTN_SLOTS

    def group(g, carry):
        for k in range(ATTN_SLOTS):
            step(first + g * ATTN_SLOTS + k, (first + k) % ATTN_SLOTS)
        return carry

    lax.fori_loop(0, n_groups, group, 0)
    for j in range(first + n_groups * ATTN_SLOTS, n_chunks):
        step(j, j % ATTN_SLOTS)
    numerators((n_chunks - 1) % ATTN_SLOTS)
    outputs(n_chunks - 2, (n_chunks - 2) % ATTN_SLOTS)
    outputs(n_chunks - 1, (n_chunks - 1) % ATTN_SLOTS)


def _attn(q_arrays, kv_arrays, mk, mv, bias, mask_a, sink, sel_a, sel_b, *, sq, n_chunks, off_a, off_b):
    b = q_arrays[0][0].shape[0]
    tq = n_chunks * CHUNK
    q_widths = (A_Q, B_W, C_W)
    kv_widths = (A_KV, A_KV, B_W, B_W)
    in_specs, args = [], []
    for (arr, cb), w in zip(q_arrays, q_widths):
        in_specs.append(pl.BlockSpec((1, tq, w), lambda i, t, cb=cb: (i, t, cb)))
        args.append(arr)
    for (arr, cb), w in zip(kv_arrays, kv_widths):
        in_specs.append(pl.BlockSpec((1, arr.shape[1], w), lambda i, t, cb=cb: (i, 0, cb)))
        args.append(arr)
    for arr in (mk, mv):
        in_specs.append(pl.BlockSpec((1, N_MEM, C_W), lambda i, t: (i, 0, 0)))
        args.append(arr)
    for arr in (bias, mask_a, sink, sel_a, sel_b):
        in_specs.append(_const_spec(arr.shape))
        args.append(arr)
    return pl.pallas_call(
        functools.partial(_attn_kernel, n_chunks=n_chunks, off_a=off_a, off_b=off_b),
        grid=(b, sq // tq),
        in_specs=in_specs,
        out_specs=pl.BlockSpec((1, tq, D_MODEL), lambda i, t: (i, t, 0)),
        out_shape=jax.ShapeDtypeStruct((b, sq, D_MODEL), BF16),
        scratch_shapes=[pltpu.VMEM((ATTN_SLOTS, rows, cols), dtype)
                        for dtype, cols_of in ((F32, (A_SCORE_COLS, B_KEYS, N_MEM)), (BF16, (A_SCORE_COLS, B_KEYS, N_MEM)),
                                               (F32, (V7X_LANES,) * 3))
                        for rows, cols in zip((A_ROWS, B_ROWS, C_ROWS), cols_of)],
        compiler_params=_params("parallel", "arbitrary"),
        name="attn",
    )(*args)


def _merge_kernel(h_ref, y_ref, x_ref, wgate_ref, bgate_ref, wa_ref, wb_ref, wc_ref, wout_ref, o_ref):
    sub = TOKEN_TILE // MERGE_SUBTILES
    for t in range(MERGE_SUBTILES):
        rows = slice(t * sub, (t + 1) * sub)
        h = h_ref[rows, :]
        merged = None
        col = 0
        for i, (w_ref, width) in enumerate(((wa_ref, A_Q), (wb_ref, B_W), (wc_ref, C_W))):
            gate = _sigmoid(_dot(h, wgate_ref[:, i * D_MODEL:(i + 1) * D_MODEL])
                            + bgate_ref[:, i * D_MODEL:(i + 1) * D_MODEL])
            term = gate * _dot(y_ref[rows, col:col + width], w_ref[...])
            merged = term if merged is None else merged + term
            col += width
        o_ref[rows, :] = x_ref[rows, :] + _dot(merged.astype(BF16), wout_ref[...])


def _merge(h, y, x, w_gate, b_gate, wa, wb, wc, w_out):
    n = h.shape[0]
    assert n % TOKEN_TILE == 0, n
    tile = pl.BlockSpec((TOKEN_TILE, D_MODEL), lambda i: (i, 0))
    consts = (w_gate, b_gate, wa, wb, wc, w_out)
    return pl.pallas_call(
        _merge_kernel,
        grid=(n // TOKEN_TILE,),
        in_specs=[tile, tile, tile] + [_const_spec(c.shape) for c in consts],
        out_specs=tile,
        out_shape=jax.ShapeDtypeStruct((n, D_MODEL), F32),
        compiler_params=_params("parallel"),
        name="merge",
    )(h, y, x, *consts)


def _rope_tables(pos):
    half = HEAD_DIM // 2
    inv = ROPE_THETA ** (-jnp.arange(half, dtype=F32) / half)
    ang = pos.astype(F32)[:, None] * inv[None, :]
    cos, sin = jnp.cos(ang), jnp.sin(ang)
    reps = V7X_LANES // HEAD_DIM
    return jnp.tile(jnp.concatenate([cos, cos], -1), (1, reps)), jnp.tile(jnp.concatenate([-sin, sin], -1), (1, reps))


def _bias_tables(rel_bias):
    rel_bias = rel_bias.astype(F32)
    near_lags = (REL_CLIP + CHUNK - 1) // CHUNK + 1
    ext = jnp.pad(rel_bias, ((0, 0), (0, near_lags * CHUNK - REL_CLIP)), mode="edge")
    period = 2 * CHUNK
    table = jnp.broadcast_to(rel_bias[:, None, -1:], (B_HEADS, CHUNK, B_KEYS))
    lag = jnp.arange(B_BAND_CHUNKS)[:, None, None, None] - (jnp.arange(B_KEYS) // CHUNK)[None, None, None, :]
    table = jnp.broadcast_to(table[None], (B_BAND_CHUNKS,) + table.shape)
    for g in range(near_lags):
        seg = ext[:, g * CHUNK + REL_CLIP - CHUNK + 1:g * CHUNK + REL_CLIP + CHUNK]
        ring = jnp.concatenate([seg[:, CHUNK - 1::-1], jnp.zeros((B_HEADS, 1), F32), seg[:, :CHUNK - 1:-1]], axis=1)
        block = jnp.tile(ring, (1, CHUNK))[:, :CHUNK * (period - 1)].reshape(B_HEADS, CHUNK, period - 1)[:, :, :CHUNK]
        table = jnp.where(lag == g, jnp.tile(block, (1, 1, B_BAND_CHUNKS))[None], table)
    visible = (lag >= 0) & (lag <= B_PREV_CHUNKS)
    return jnp.where(visible, table, NEG).reshape(B_BAND_CHUNKS, B_HEADS * CHUNK, B_KEYS)


def _mask_a_tables():
    j = jnp.arange(A_KEYS)[None, :]
    rows = [jnp.where((j // CHUNK) <= v, 0.0, NEG) for v in range(A_PREV_CHUNKS + 1)]
    return jnp.stack([jnp.broadcast_to(r, (8, A_KEYS)) for r in rows]).astype(F32)


def _block_diag_ones():
    i = jnp.arange(V7X_MXU_DIM)
    return (i[:, None] // HEAD_DIM == i[None, :] // HEAD_DIM).astype(BF16)


def _lane_selectors():
    lane = jnp.arange(V7X_LANES)[None, :] // HEAD_DIM
    sel_a = jnp.stack([jnp.broadcast_to(lane == p, (CHUNK, V7X_LANES)) for p in range(2)]).astype(BF16)
    lane = jnp.arange(B_W)[None, :] // HEAD_DIM
    sel_b = jnp.stack([jnp.broadcast_to(lane == h, (CHUNK, B_W)) for h in range(B_HEADS)]).astype(BF16)
    return sel_a, sel_b


def kernel(x_prompt, x_sample, cache_a_k, cache_a_v, cache_b_k, cache_b_v, cache_mem_k, cache_mem_v,
           mem_prompt, g_ff1, w_ff1_gate, w_ff1_up, w_ff1_down, g_mix, w_in, g_qa, g_ka, sinks_a,
           g_qb, g_kb, rel_bias_b, g_qc, g_mem, w_mem_kv, g_kc, w_gate, b_gate, w_br_a, w_br_b,
           w_br_c, w_out, g_ff2, w_ff2_gate, w_ff2_up, w_ff2_down, g_final):
    bp, sp, _ = x_prompt.shape
    bs, ss, _ = x_sample.shape
    l = 0
    row = lambda g: g[l].reshape(1, -1).astype(F32)
    bf = lambda w: w[l].astype(BF16)

    order = jnp.array(A_HEAD_ORDER)
    head_cols = (order[:, None] * HEAD_DIM + jnp.arange(HEAD_DIM)[None, :]).reshape(-1)
    w_in_l = w_in[l]
    w_in_bf = jnp.concatenate([w_in_l[:, :A_Q][:, head_cols], w_in_l[:, A_Q:]], axis=1).astype(BF16)
    w_br_a_bf = w_br_a[l][head_cols].astype(BF16)
    scale = HEAD_DIM ** -0.5 * LOG2E
    ones = lambda n: jnp.ones((n,), F32)
    gain = jnp.concatenate([
        jnp.tile(g_qa[l], A_Q_HEADS) * scale, jnp.tile(g_ka[l], A_KV_HEADS), ones(A_KV),
        jnp.tile(g_qb[l], B_HEADS) * scale, jnp.tile(g_kb[l], B_HEADS), ones(B_W),
        jnp.tile(g_qc[l], C_HEADS) * scale]).reshape(1, IN_COLS).astype(F32)
    sink = jnp.concatenate([jnp.repeat(sinks_a[l][order].astype(F32) * LOG2E, CHUNK)[:, None],
                            jnp.full((A_ROWS, A_SCORE_COLS - A_KEYS - 1), NEG, F32)], axis=1)
    bias = _bias_tables(rel_bias_b[l] * LOG2E)
    mask_a = _mask_a_tables()
    bd = _block_diag_ones()
    sel_a, sel_b = _lane_selectors()
    ff1 = (row(g_ff1), bf(w_ff1_gate), bf(w_ff1_up), bf(w_ff1_down), row(g_mix))
    ff2 = (row(g_ff2), bf(w_ff2_gate), bf(w_ff2_up), bf(w_ff2_down), row(g_final))
    merge_w = (bf(w_gate), row(b_gate), w_br_a_bf, bf(w_br_b), bf(w_br_c), bf(w_out))

    def trunk(x, pos, batch_view, attend, keep):
        n = x.shape[0] * x.shape[1]
        xf = x.reshape(n, D_MODEL)
        x1, h = _ffn(xf, *ff1, emit_x=True, norm_dtype=BF16)
        cos, sin = _rope_tables(pos)
        q, ka16, va16, kb16, vb16, *kv32 = _proj(h.reshape(batch_view + (D_MODEL,)), w_in_bf, gain, cos, sin, bd, keep)
        y = attend(q, ka16, va16, kb16, vb16)
        x2 = _merge(h, y.reshape(n, D_MODEL), x1, *merge_w)
        (out,) = _ffn(x2, *ff2, emit_x=False, norm_dtype=F32)
        return out.reshape(x.shape), kv32

    def split_kv(kv32, b, rows):
        kv32 = kv32.reshape(b, rows, KV_COLS)
        ka = kv32[..., :A_KV].reshape(b, rows, A_KV_HEADS, HEAD_DIM)
        va = kv32[..., A_KV:2 * A_KV].reshape(b, rows, A_KV_HEADS, HEAD_DIM)
        kb = kv32[..., 2 * A_KV:2 * A_KV + B_W].reshape(b, rows, B_HEADS, HEAD_DIM)
        vb = kv32[..., 2 * A_KV + B_W:].reshape(b, rows, B_HEADS, HEAD_DIM)
        return ka, va, kb, vb

    g_kc_row = jnp.tile(g_kc[l], C_HEADS).reshape(1, C_W).astype(F32)
    mk32t, mv32t, mem16 = _memkv(mem_prompt.reshape(bp * N_MEM, D_MODEL), row(g_mem), bf(w_mem_kv), g_kc_row, bd)
    mem16 = mem16.reshape(bp, N_MEM, 2 * C_W)
    heads_last = lambda a, n_heads: a.reshape(a.shape[0], n_heads, HEAD_DIM, a.shape[-1]).transpose(0, 3, 1, 2)[None]
    mk_p = heads_last(mk32t, C_HEADS)
    mv_p = heads_last(mv32t, C_HEADS)

    q_blocks = (0, A_Q // B_W, (A_Q + B_W) // C_W)

    def attend_prompt(q, ka16, va16, kb16, vb16):
        q_arrays = tuple((q, cb) for cb in q_blocks)
        kv_arrays = ((ka16, 0), (va16, 0), (kb16, 0), (vb16, 0))
        return _attn(q_arrays, kv_arrays, mem16[..., :C_W], mem16[..., C_W:], bias, mask_a, sink, sel_a, sel_b,
                     sq=sp, n_chunks=min(ATTN_CHUNKS_PER_STEP, sp // CHUNK), off_a=0, off_b=0)

    keep = (min(A_PREV_CHUNKS * CHUNK, sp), min(B_PREV_CHUNKS * CHUNK, sp))
    y_p, tails = trunk(x_prompt, jnp.arange(sp, dtype=jnp.int32), (bp, sp), attend_prompt, keep)
    prompt_caches = tuple(heads_last(a, n_heads) for a, n_heads in zip(tails, (A_KV_HEADS, A_KV_HEADS, B_HEADS, B_HEADS)))

    n_s = bs * ss
    rows_per_tile = TOKEN_TILE // ss
    pos_s = PAST_LEN + jnp.tile(jnp.arange(ss, dtype=jnp.int32), rows_per_tile)
    flat16 = lambda c: c[l].reshape(c.shape[1], c.shape[2], -1).astype(BF16)
    ca_k, ca_v, cb_k, cb_v = flat16(cache_a_k), flat16(cache_a_v), flat16(cache_b_k), flat16(cache_b_v)
    cm_k, cm_v = flat16(cache_mem_k), flat16(cache_mem_v)

    def attend_sample(q, ka16, va16, kb16, vb16):
        per_seq = lambda a: a.reshape(bs, ss, a.shape[-1])
        lead = jnp.zeros((bs, B_KEYS - cb_k.shape[1] - ss, B_W), BF16)
        kv_arrays = ((jnp.concatenate([ca_k, per_seq(ka16)], 1), 0), (jnp.concatenate([ca_v, per_seq(va16)], 1), 0),
                     (jnp.concatenate([lead, cb_k, per_seq(kb16)], 1), 0),
                     (jnp.concatenate([lead, cb_v, per_seq(vb16)], 1), 0))
        q_arrays = tuple((per_seq(q), cb) for cb in q_blocks)
        return _attn(q_arrays, kv_arrays, cm_k, cm_v, bias, mask_a, sink, sel_a, sel_b,
                     sq=ss, n_chunks=1, off_a=ca_k.shape[1] // CHUNK, off_b=(B_KEYS - ss) // CHUNK)

    y_s, (kv32_s,) = trunk(x_sample, pos_s, (n_s // TOKEN_TILE, TOKEN_TILE), attend_sample, None)
    ka_s, va_s, kb_s, vb_s = split_kv(kv32_s, bs, ss)

    return (y_p, y_s, *prompt_caches, mk_p, mv_p, ka_s[None], va_s[None], kb_s[None], vb_s[None])
```

```python
import functools

import jax
import jax.numpy as jnp
from jax import lax
from jax.experimental import pallas as pl
from jax.experimental.pallas import tpu as pltpu

D_MODEL = 1024
PAST_LEN = 1024
CHUNK = 64
HEAD_DIM = 64
A_Q_HEADS = 8
A_KV_HEADS = 2
A_GROUP = A_Q_HEADS // A_KV_HEADS
A_PREV_CHUNKS = 2
B_HEADS = 4
B_PREV_CHUNKS = 8
REL_CLIP = 128
C_HEADS = 4
N_MEM = 256
FF_DIM = 2816
ROPE_THETA = 10000.0
EPS = 1e-6
NEG = -1e30

A_Q = A_Q_HEADS * HEAD_DIM
A_KV = A_KV_HEADS * HEAD_DIM
B_W = B_HEADS * HEAD_DIM
C_W = C_HEADS * HEAD_DIM
IN_COLS = A_Q + 2 * A_KV + 3 * B_W + C_W
A_KEYS = (A_PREV_CHUNKS + 1) * CHUNK
B_BAND_CHUNKS = B_PREV_CHUNKS + 2
B_KEYS = B_BAND_CHUNKS * CHUNK
OFF_QA, OFF_KA, OFF_VA = 0, A_Q, A_Q + A_KV
OFF_QB, OFF_KB, OFF_VB = A_Q + 2 * A_KV, A_Q + 2 * A_KV + B_W, A_Q + 2 * A_KV + 2 * B_W
OFF_QC = A_Q + 2 * A_KV + 3 * B_W
KV_COLS = 2 * A_KV + 2 * B_W
A_HEAD_ORDER = (0, 4, 1, 5, 2, 6, 3, 7)

V7X_LANES = 128
V7X_MXU_DIM = 256
V7X_VMEM_LIMIT = 56 * 1024 * 1024

TOKEN_TILE = 1024
FF_CHUNKS = (6 * V7X_MXU_DIM, 5 * V7X_MXU_DIM)
assert sum(FF_CHUNKS) == FF_DIM
FFN_SUBTILES = 4
PROJ_SUBTILES = 4
MERGE_SUBTILES = 4
ATTN_CHUNKS_PER_STEP = 32
ATTN_SLOTS = 4
LOG2E = 1.4426950408889634
BF16 = jnp.bfloat16
F32 = jnp.float32


def _dot(a, b):
    return jnp.dot(a, b, preferred_element_type=F32)


def _dot_nt(a, b):
    return lax.dot_general(a, b, (((1,), (1,)), ((), ())), preferred_element_type=F32)


def _rmsnorm(x, g):
    return x * lax.rsqrt(jnp.mean(x * x, axis=-1, keepdims=True) + EPS) * g


def _sigmoid(z):
    return 1.0 / (1.0 + jnp.exp(-z))


def _const_spec(shape):
    nd = len(shape)
    return pl.BlockSpec(shape, lambda *_: (0,) * nd, pipeline_mode=pl.Buffered(1))


def _params(*sem):
    return pltpu.CompilerParams(dimension_semantics=sem, vmem_limit_bytes=V7X_VMEM_LIMIT)


def _swiglu_residual(x, g_pre, wg_ref, wu_ref, wd_ref):
    hn = _rmsnorm(x, g_pre).astype(BF16)
    acc = None
    lo = 0
    for width in FF_CHUNKS:
        g = _dot(hn, wg_ref[:, lo:lo + width])
        u = _dot(hn, wu_ref[:, lo:lo + width])
        a = (g * _sigmoid(g) * u).astype(BF16)
        d = _dot(a, wd_ref[lo:lo + width, :])
        acc = d if acc is None else acc + d
        lo += width
    return x + 0.5 * acc


def _ffn_kernel(x_ref, gpre_ref, wg_ref, wu_ref, wd_ref, gpost_ref, *out_refs, emit_x):
    n_ref = out_refs[-1]
    sub = TOKEN_TILE // FFN_SUBTILES
    for i in range(FFN_SUBTILES):
        rows = slice(i * sub, (i + 1) * sub)
        y = _swiglu_residual(x_ref[rows, :], gpre_ref[...], wg_ref, wu_ref, wd_ref)
        if emit_x:
            out_refs[0][rows, :] = y
        n_ref[rows, :] = _rmsnorm(y, gpost_ref[...]).astype(n_ref.dtype)


def _ffn(x, g_pre, wg, wu, wd, g_post, *, emit_x, norm_dtype):
    n = x.shape[0]
    assert n % TOKEN_TILE == 0, n
    tile = pl.BlockSpec((TOKEN_TILE, D_MODEL), lambda i: (i, 0))
    out_shape = [jax.ShapeDtypeStruct((n, D_MODEL), norm_dtype)]
    out_specs = [tile]
    if emit_x:
        out_shape.insert(0, jax.ShapeDtypeStruct((n, D_MODEL), F32))
        out_specs.insert(0, tile)
    return pl.pallas_call(
        functools.partial(_ffn_kernel, emit_x=emit_x),
        grid=(n // TOKEN_TILE,),
        in_specs=[tile, _const_spec((1, D_MODEL)), _const_spec(wg.shape), _const_spec(wu.shape),
                  _const_spec(wd.shape), _const_spec((1, D_MODEL))],
        out_specs=out_specs,
        out_shape=out_shape,
        compiler_params=_params("parallel"),
        name="ffn_x" if emit_x else "ffn_final",
    )(x, g_pre, wg, wu, wd, g_post)


def _head_inv_rms(y, bd):
    w = y.shape[1]
    ssq = _dot((y * y).astype(BF16), bd[:w, :w])
    return lax.rsqrt(ssq * (1.0 / HEAD_DIM) + EPS)


def _rope(x, cos, sin_signed, first_half):
    rot = jnp.where(first_half, pltpu.roll(x, V7X_LANES - HEAD_DIM // 2, 1), pltpu.roll(x, HEAD_DIM // 2, 1))
    return x * cos + rot * sin_signed


def _proj_kernel(h_ref, w_ref, gain_ref, cos_ref, sin_ref, bd_ref, q_ref, ka_ref, va_ref, kb_ref, vb_ref, *f32_refs,
                 n_tiles, keep):
    bd = bd_ref[...]
    stash_rows = 0 if keep is None else max(keep)
    stash_ref = None if keep is None else f32_refs[-1]
    sub = TOKEN_TILE // PROJ_SUBTILES
    lane = lax.broadcasted_iota(jnp.int32, (sub, V7X_LANES), 1)
    first_half = (lane % HEAD_DIM) < (HEAD_DIM // 2)
    for i in range(PROJ_SUBTILES):
        rows = slice(i * sub, (i + 1) * sub)
        y = _dot(h_ref[0, rows, :], w_ref[...])
        cos = cos_ref[rows, :]
        sin = sin_ref[rows, :]

        def normed(off, width):
            blk = y[:, off:off + width]
            return blk * _head_inv_rms(blk, bd) * gain_ref[:, off:off + width]

        pieces = {}
        for off in range(OFF_QA, OFF_QA + A_Q, V7X_MXU_DIM):
            n = normed(off, V7X_MXU_DIM)
            for s in range(0, V7X_MXU_DIM, V7X_LANES):
                pieces[off + s] = _rope(n[:, s:s + V7X_LANES], cos, sin, first_half)
        pieces[OFF_KA] = _rope(normed(OFF_KA, A_KV), cos, sin, first_half)
        pieces[OFF_VA] = y[:, OFF_VA:OFF_VA + A_KV]
        for off in (OFF_QB, OFF_KB, OFF_QC):
            n = normed(off, V7X_MXU_DIM)
            for s in range(0, V7X_MXU_DIM, V7X_LANES):
                pieces[off + s] = n[:, s:s + V7X_LANES]
        for s in range(0, B_W, V7X_LANES):
            pieces[OFF_VB + s] = y[:, OFF_VB + s:OFF_VB + s + V7X_LANES]

        q_col = 0
        for off, width in ((OFF_QA, A_Q), (OFF_QB, B_W), (OFF_QC, C_W)):
            for s in range(0, width, V7X_LANES):
                q_ref[0, rows, q_col:q_col + V7X_LANES] = pieces[off + s].astype(BF16)
                q_col += V7X_LANES
        tail_lo = max(i * sub, TOKEN_TILE - stash_rows)
        col = 0
        for o_ref, off, width in ((ka_ref, OFF_KA, A_KV), (va_ref, OFF_VA, A_KV), (kb_ref, OFF_KB, B_W), (vb_ref, OFF_VB, B_W)):
            for s in range(0, width, V7X_LANES):
                o_ref[0, rows, s:s + V7X_LANES] = pieces[off + s].astype(BF16)
                if keep is None:
                    f32_refs[0][0, rows, col:col + V7X_LANES] = pieces[off + s]
                elif tail_lo < (i + 1) * sub:
                    stash_ref[tail_lo - (TOKEN_TILE - stash_rows):(i + 1) * sub - (TOKEN_TILE - stash_rows),
                              col:col + V7X_LANES] = pieces[off + s][tail_lo - i * sub:]
                col += V7X_LANES

    if keep is None:
        return

    def write_tails():
        col = 0
        for o_ref, width, n_rows in zip(f32_refs[:4], (A_KV, A_KV, B_W, B_W), (keep[0], keep[0], keep[1], keep[1])):
            for s in range(0, width, V7X_LANES):
                o_ref[0, s:s + V7X_LANES, :] = stash_ref[stash_rows - n_rows:, col:col + V7X_LANES].T
                col += V7X_LANES

    if n_tiles == 1:
        write_tails()
    else:
        pl.when(pl.program_id(1) == n_tiles - 1)(write_tails)


def _proj(h, w_in, gain, cos, sin, bd, keep=None):
    b, s, _ = h.shape
    assert s % TOKEN_TILE == 0, s
    widths = (A_Q + B_W + C_W, A_KV, A_KV, B_W, B_W)
    out_specs = [pl.BlockSpec((1, TOKEN_TILE, w), lambda i, t: (i, t, 0)) for w in widths]
    out_shape = [jax.ShapeDtypeStruct((b, s, w), BF16) for w in widths]
    scratch = []
    if keep is None:
        out_specs.append(pl.BlockSpec((1, TOKEN_TILE, KV_COLS), lambda i, t: (i, 0, 0)))
        out_shape.append(jax.ShapeDtypeStruct((b, TOKEN_TILE, KV_COLS), F32))
    else:
        assert max(keep) <= TOKEN_TILE
        for w, n_rows in zip((A_KV, A_KV, B_W, B_W), (keep[0], keep[0], keep[1], keep[1])):
            out_specs.append(pl.BlockSpec((1, w, n_rows), lambda i, t: (i, 0, 0)))
            out_shape.append(jax.ShapeDtypeStruct((b, w, n_rows), F32))
        scratch.append(pltpu.VMEM((max(keep), KV_COLS), F32))
    return pl.pallas_call(
        functools.partial(_proj_kernel, n_tiles=s // TOKEN_TILE, keep=keep),
        grid=(b, s // TOKEN_TILE),
        in_specs=[pl.BlockSpec((1, TOKEN_TILE, D_MODEL), lambda i, t: (i, t, 0)),
                  _const_spec(w_in.shape), _const_spec(gain.shape),
                  pl.BlockSpec((TOKEN_TILE, V7X_LANES), lambda i, t: (t, 0)),
                  pl.BlockSpec((TOKEN_TILE, V7X_LANES), lambda i, t: (t, 0)),
                  _const_spec(bd.shape)],
        out_specs=out_specs,
        out_shape=out_shape,
        scratch_shapes=scratch,
        compiler_params=_params("parallel", "arbitrary"),
        name="proj",
    )(h, w_in, gain, cos, sin, bd)


def _memkv_kernel(m_ref, gmem_ref, w_ref, gkc_ref, bd_ref, k32t_ref, v32t_ref, k16_ref, v16_ref):
    hn = _rmsnorm(m_ref[...], gmem_ref[...]).astype(BF16)
    y = _dot(hn, w_ref[...])
    k = y[:, :C_W]
    k = k * _head_inv_rms(k, bd_ref[...]) * gkc_ref[...]
    v = y[:, C_W:]
    k16_ref[...] = k.astype(BF16)
    v16_ref[...] = v.astype(BF16)
    for e in range(TOKEN_TILE // N_MEM):
        tok = slice(e * N_MEM, (e + 1) * N_MEM)
        k32t_ref[e] = k[tok].T
        v32t_ref[e] = v[tok].T


def _memkv(mem, g_mem, w, g_kc, bd):
    n = mem.shape[0]
    assert n % TOKEN_TILE == 0 and TOKEN_TILE % N_MEM == 0, n
    seqs = TOKEN_TILE // N_MEM
    return pl.pallas_call(
        _memkv_kernel,
        grid=(n // TOKEN_TILE,),
        in_specs=[pl.BlockSpec((TOKEN_TILE, D_MODEL), lambda i: (i, 0)), _const_spec((1, D_MODEL)),
                  _const_spec(w.shape), _const_spec((1, C_W)), _const_spec(bd.shape)],
        out_specs=[pl.BlockSpec((seqs, C_W, N_MEM), lambda i: (i, 0, 0))] * 2
                  + [pl.BlockSpec((TOKEN_TILE, C_W), lambda i: (i, 0))] * 2,
        out_shape=[jax.ShapeDtypeStruct((n // N_MEM, C_W, N_MEM), F32)] * 2
                  + [jax.ShapeDtypeStruct((n, C_W), BF16)] * 2,
        compiler_params=_params("parallel"),
        name="memkv",
    )(mem, g_mem, w, g_kc, bd)


A_ROWS = A_Q_HEADS * CHUNK
A_SCORE_COLS = 2 * V7X_LANES
B_ROWS = B_HEADS * CHUNK
C_ROWS = C_HEADS * CHUNK


def _attn_kernel(qa_ref, qb_ref, qc_ref, ka_ref, va_ref, kb_ref, vb_ref, mk_ref, mv_ref,
                 bias_ref, maska_ref, sinkpad_ref, sela_ref, selb_ref, y_ref,
                 sa_ref, sb_ref, sc_ref, pa_ref, pb_ref, pc_ref, la_ref, lb_ref, lc_ref,
                 *, n_chunks, off_a, off_b):
    t = pl.program_id(1)
    lane_a = lax.broadcasted_iota(jnp.int32, (CHUNK, V7X_LANES), 1)
    low_half = lane_a < HEAD_DIM
    lane_b = lax.broadcasted_iota(jnp.int32, (CHUNK, B_W), 1) // HEAD_DIM

    def stack_heads(q, sel_ref, n_heads):
        return jnp.concatenate([q * sel_ref[h] for h in range(n_heads)], axis=0)

    def unstack_heads(r, n_heads):
        out = r[(n_heads - 1) * CHUNK:]
        for h in reversed(range(n_heads - 1)):
            out = jnp.where(lane_b == h, r[h * CHUNK:(h + 1) * CHUNK], out)
        return out

    def rows_of(j):
        return pl.ds(pl.multiple_of(j * CHUNK, CHUNK), CHUNK)

    def band(j, off, prev_chunks):
        c = t * n_chunks + j + off
        return pl.multiple_of(jnp.maximum(c - prev_chunks, 0) * CHUNK, CHUNK), jnp.minimum(c, prev_chunks)

    def scores(j, slot, first_chunks):
        rows = rows_of(j)
        start_a, var_a = band(j, off_a, A_PREV_CHUNKS)
        qa = qa_ref[0, rows, :]
        lhs = jnp.concatenate(
            [qa[:, (p // 2) * V7X_LANES:(p // 2 + 1) * V7X_LANES] * sela_ref[p % 2] for p in range(A_Q_HEADS)],
            axis=0)
        s = _dot_nt(lhs, ka_ref[0, pl.ds(start_a, A_KEYS), :])
        if first_chunks:
            s = s + maska_ref[var_a][0:1]
        sa_ref[slot, :, :A_KEYS] = s
        start_b, var_b = band(j, off_b, B_BAND_CHUNKS - 1)
        lhs = stack_heads(qb_ref[0, rows, :], selb_ref, B_HEADS)
        sb_ref[slot] = _dot_nt(lhs, kb_ref[0, pl.ds(start_b, B_KEYS), :]) + bias_ref[var_b]
        lhs = stack_heads(qc_ref[0, rows, :], selb_ref, C_HEADS)
        sc_ref[slot] = _dot_nt(lhs, mk_ref[0])

    def numerators(slot):
        for s_ref, p_ref, l_ref in ((sa_ref, pa_ref, la_ref), (sb_ref, pb_ref, lb_ref), (sc_ref, pc_ref, lc_ref)):
            s = s_ref[slot]
            e = jnp.exp2(s - jnp.max(s, axis=-1, keepdims=True))
            p_ref[slot] = e.astype(BF16)
            l_ref[slot] = jnp.broadcast_to(1.0 / jnp.sum(e, axis=-1, keepdims=True), l_ref.shape[1:])

    def outputs(j, slot):
        rows = rows_of(j)
        start_a, _ = band(j, off_a, A_PREV_CHUNKS)
        r = _dot(pa_ref[slot, :, :A_KEYS], va_ref[0, pl.ds(start_a, A_KEYS), :]) * la_ref[slot]
        for jc in range(A_Q // V7X_LANES):
            ev = r[(2 * jc) * CHUNK:(2 * jc + 1) * CHUNK]
            od = r[(2 * jc + 1) * CHUNK:(2 * jc + 2) * CHUNK]
            y_ref[0, rows, jc * V7X_LANES:(jc + 1) * V7X_LANES] = jnp.where(low_half, ev, od).astype(BF16)
        start_b, _ = band(j, off_b, B_BAND_CHUNKS - 1)
        inv_l = lb_ref[slot]
        r = _dot(pb_ref[slot], vb_ref[0, pl.ds(start_b, B_KEYS), :]) * jnp.concatenate([inv_l, inv_l], axis=1)
        y_ref[0, rows, A_Q:A_Q + B_W] = unstack_heads(r, B_HEADS).astype(BF16)
        inv_l = lc_ref[slot]
        r = _dot(pc_ref[slot], mv_ref[0]) * jnp.concatenate([inv_l, inv_l], axis=1)
        y_ref[0, rows, A_Q + B_W:] = unstack_heads(r, C_HEADS).astype(BF16)

    for slot in range(ATTN_SLOTS):
        sa_ref[slot, :, A_KEYS:] = sinkpad_ref[...]

    if n_chunks < ATTN_SLOTS:
        for j in range(n_chunks):
            scores(j, 0, j + off_a < A_PREV_CHUNKS)
            numerators(0)
            outputs(j, 0)
        return

    assert A_PREV_CHUNKS <= 2

    def step(j, slot):
        scores(j, slot, False)
        numerators((slot - 1) % ATTN_SLOTS)
        outputs(j - 2, (slot - 2) % ATTN_SLOTS)

    scores(0, 0, True)
    scores(1, 1, True)
    numerators(0)
    first = 2
    n_groups = (n_chunks - first) // ATTN_SLOTS

    def group(g, carry):
        for k in range(ATTN_SLOTS):
            step(first + g * ATTN_SLOTS + k, (first + k) % ATTN_SLOTS)
        return carry

    lax.fori_loop(0, n_groups, group, 0)
    for j in range(first + n_groups * ATTN_SLOTS, n_chunks):
        step(j, j % ATTN_SLOTS)
    numerators((n_chunks - 1) % ATTN_SLOTS)
    outputs(n_chunks - 2, (n_chunks - 2) % ATTN_SLOTS)
    outputs(n_chunks - 1, (n_chunks - 1) % ATTN_SLOTS)


def _attn_cached_kernel(qa_ref, qb_ref, qc_ref, ka_new, va_new, kb_new, vb_new, cak_ref, cav_ref, cbk_ref, cbv_ref,
                        cmk_ref, cmv_ref, bias_ref, maska_ref, sinkpad_ref, sela_ref, selb_ref, y_ref, *scratch,
                        off_a, off_b):
    *stage_scratch, ka_s, va_s, kb_s, vb_s, mk_s, mv_s = scratch
    for band_ref, cache_ref, new_ref in ((ka_s, cak_ref, ka_new), (va_s, cav_ref, va_new),
                                         (kb_s, cbk_ref, kb_new), (vb_s, cbv_ref, vb_new)):
        n_hist, n_new = cache_ref.shape[2], new_ref.shape[1]
        lead = band_ref.shape[1] - n_hist - n_new
        if lead:
            band_ref[0, :lead, :] = jnp.zeros((lead, band_ref.shape[2]), BF16)
        band_ref[0, lead:lead + n_hist, :] = cache_ref[0].T.astype(BF16)
        band_ref[0, lead + n_hist:, :] = new_ref[0]
    mk_s[0] = cmk_ref[0].T.astype(BF16)
    mv_s[0] = cmv_ref[0].T.astype(BF16)
    _attn_kernel(qa_ref, qb_ref, qc_ref, ka_s, va_s, kb_s, vb_s, mk_s, mv_s, bias_ref, maska_ref, sinkpad_ref,
                 sela_ref, selb_ref, y_ref, *stage_scratch, n_chunks=1, off_a=off_a, off_b=off_b)


def _attn(q_arrays, kv_arrays, mem_kv, bias, mask_a, sink, sel_a, sel_b, *, sq, n_chunks, off_a, off_b, caches=None):
    b = q_arrays[0][0].shape[0]
    tq = n_chunks * CHUNK
    in_specs, args = [], []
    for (arr, cb), w in zip(q_arrays, (A_Q, B_W, C_W)):
        in_specs.append(pl.BlockSpec((1, tq, w), lambda i, t, cb=cb: (i, t, cb)))
        args.append(arr)
    for arr in tuple(kv_arrays) + tuple(mem_kv if caches is None else caches):
        in_specs.append(pl.BlockSpec((1,) + arr.shape[1:], lambda i, t: (i, 0, 0)))
        args.append(arr)
    for arr in (bias, mask_a, sink, sel_a, sel_b):
        in_specs.append(_const_spec(arr.shape))
        args.append(arr)
    scratch = [pltpu.VMEM((ATTN_SLOTS, rows, cols), dtype)
               for dtype, cols_of in ((F32, (A_SCORE_COLS, B_KEYS, N_MEM)), (BF16, (A_SCORE_COLS, B_KEYS, N_MEM)),
                                      (F32, (V7X_LANES,) * 3))
               for rows, cols in zip((A_ROWS, B_ROWS, C_ROWS), cols_of)]
    if caches is None:
        body = functools.partial(_attn_kernel, n_chunks=n_chunks, off_a=off_a, off_b=off_b)
    else:
        assert n_chunks == 1
        body = functools.partial(_attn_cached_kernel, off_a=off_a, off_b=off_b)
        scratch += [pltpu.VMEM((1, rows, cols), BF16)
                    for rows, cols in ((A_KEYS, A_KV), (A_KEYS, A_KV), (B_KEYS, B_W), (B_KEYS, B_W), (N_MEM, C_W), (N_MEM, C_W))]
    return pl.pallas_call(
        body,
        grid=(b, sq // tq),
        in_specs=in_specs,
        out_specs=pl.BlockSpec((1, tq, D_MODEL), lambda i, t: (i, t, 0)),
        out_shape=jax.ShapeDtypeStruct((b, sq, D_MODEL), BF16),
        scratch_shapes=scratch,
        compiler_params=_params("parallel", "arbitrary"),
        name="attn",
    )(*args)


def _merge_kernel(h_ref, y_ref, x_ref, wgate_ref, bgate_ref, wa_ref, wb_ref, wc_ref, wout_ref, o_ref):
    sub = TOKEN_TILE // MERGE_SUBTILES
    for t in range(MERGE_SUBTILES):
        rows = slice(t * sub, (t + 1) * sub)
        h = h_ref[rows, :]
        merged = None
        col = 0
        for i, (w_ref, width) in enumerate(((wa_ref, A_Q), (wb_ref, B_W), (wc_ref, C_W))):
            gate = _sigmoid(_dot(h, wgate_ref[:, i * D_MODEL:(i + 1) * D_MODEL])
                            + bgate_ref[:, i * D_MODEL:(i + 1) * D_MODEL])
            term = gate * _dot(y_ref[rows, col:col + width], w_ref[...])
            merged = term if merged is None else merged + term
            col += width
        o_ref[rows, :] = x_ref[rows, :] + _dot(merged.astype(BF16), wout_ref[...])


def _merge(h, y, x, w_gate, b_gate, wa, wb, wc, w_out):
    n = h.shape[0]
    assert n % TOKEN_TILE == 0, n
    tile = pl.BlockSpec((TOKEN_TILE, D_MODEL), lambda i: (i, 0))
    consts = (w_gate, b_gate, wa, wb, wc, w_out)
    return pl.pallas_call(
        _merge_kernel,
        grid=(n // TOKEN_TILE,),
        in_specs=[tile, tile, tile] + [_const_spec(c.shape) for c in consts],
        out_specs=tile,
        out_shape=jax.ShapeDtypeStruct((n, D_MODEL), F32),
        compiler_params=_params("parallel"),
        name="merge",
    )(h, y, x, *consts)


def _rope_tables(pos):
    half = HEAD_DIM // 2
    inv = ROPE_THETA ** (-jnp.arange(half, dtype=F32) / half)
    ang = pos.astype(F32)[:, None] * inv[None, :]
    cos, sin = jnp.cos(ang), jnp.sin(ang)
    reps = V7X_LANES // HEAD_DIM
    return jnp.tile(jnp.concatenate([cos, cos], -1), (1, reps)), jnp.tile(jnp.concatenate([-sin, sin], -1), (1, reps))


def _bias_tables(rel_bias):
    rel_bias = rel_bias.astype(F32)
    near_lags = (REL_CLIP + CHUNK - 1) // CHUNK + 1
    ext = jnp.pad(rel_bias, ((0, 0), (0, near_lags * CHUNK - REL_CLIP)), mode="edge")
    period = 2 * CHUNK
    table = jnp.broadcast_to(rel_bias[:, None, -1:], (B_HEADS, CHUNK, B_KEYS))
    lag = jnp.arange(B_BAND_CHUNKS)[:, None, None, None] - (jnp.arange(B_KEYS) // CHUNK)[None, None, None, :]
    table = jnp.broadcast_to(table[None], (B_BAND_CHUNKS,) + table.shape)
    for g in range(near_lags):
        seg = ext[:, g * CHUNK + REL_CLIP - CHUNK + 1:g * CHUNK + REL_CLIP + CHUNK]
        ring = jnp.concatenate([seg[:, CHUNK - 1::-1], jnp.zeros((B_HEADS, 1), F32), seg[:, :CHUNK - 1:-1]], axis=1)
        block = jnp.tile(ring, (1, CHUNK))[:, :CHUNK * (period - 1)].reshape(B_HEADS, CHUNK, period - 1)[:, :, :CHUNK]
        table = jnp.where(lag == g, jnp.tile(block, (1, 1, B_BAND_CHUNKS))[None], table)
    visible = (lag >= 0) & (lag <= B_PREV_CHUNKS)
    return jnp.where(visible, table, NEG).reshape(B_BAND_CHUNKS, B_HEADS * CHUNK, B_KEYS)


def _mask_a_tables():
    j = jnp.arange(A_KEYS)[None, :]
    rows = [jnp.where((j // CHUNK) <= v, 0.0, NEG) for v in range(A_PREV_CHUNKS + 1)]
    return jnp.stack([jnp.broadcast_to(r, (8, A_KEYS)) for r in rows]).astype(F32)


def _block_diag_ones():
    i = jnp.arange(V7X_MXU_DIM)
    return (i[:, None] // HEAD_DIM == i[None, :] // HEAD_DIM).astype(BF16)


def _lane_selectors():
    lane = jnp.arange(V7X_LANES)[None, :] // HEAD_DIM
    sel_a = jnp.stack([jnp.broadcast_to(lane == p, (CHUNK, V7X_LANES)) for p in range(2)]).astype(BF16)
    lane = jnp.arange(B_W)[None, :] // HEAD_DIM
    sel_b = jnp.stack([jnp.broadcast_to(lane == h, (CHUNK, B_W)) for h in range(B_HEADS)]).astype(BF16)
    return sel_a, sel_b


def kernel(x_prompt, x_sample, cache_a_k, cache_a_v, cache_b_k, cache_b_v, cache_mem_k, cache_mem_v,
           mem_prompt, g_ff1, w_ff1_gate, w_ff1_up, w_ff1_down, g_mix, w_in, g_qa, g_ka, sinks_a,
           g_qb, g_kb, rel_bias_b, g_qc, g_mem, w_mem_kv, g_kc, w_gate, b_gate, w_br_a, w_br_b,
           w_br_c, w_out, g_ff2, w_ff2_gate, w_ff2_up, w_ff2_down, g_final):
    bp, sp, _ = x_prompt.shape
    bs, ss, _ = x_sample.shape
    l = 0
    row = lambda g: g[l].reshape(1, -1).astype(F32)
    bf = lambda w: w[l].astype(BF16)

    order = jnp.array(A_HEAD_ORDER)
    head_cols = (order[:, None] * HEAD_DIM + jnp.arange(HEAD_DIM)[None, :]).reshape(-1)
    w_in_l = w_in[l]
    w_in_bf = jnp.concatenate([w_in_l[:, :A_Q][:, head_cols], w_in_l[:, A_Q:]], axis=1).astype(BF16)
    w_br_a_bf = w_br_a[l][head_cols].astype(BF16)
    scale = HEAD_DIM ** -0.5 * LOG2E
    ones = lambda n: jnp.ones((n,), F32)
    gain = jnp.concatenate([
        jnp.tile(g_qa[l], A_Q_HEADS) * scale, jnp.tile(g_ka[l], A_KV_HEADS), ones(A_KV),
        jnp.tile(g_qb[l], B_HEADS) * scale, jnp.tile(g_kb[l], B_HEADS), ones(B_W),
        jnp.tile(g_qc[l], C_HEADS) * scale]).reshape(1, IN_COLS).astype(F32)
    sink = jnp.concatenate([jnp.repeat(sinks_a[l][order].astype(F32) * LOG2E, CHUNK)[:, None],
                            jnp.full((A_ROWS, A_SCORE_COLS - A_KEYS - 1), NEG, F32)], axis=1)
    bias = _bias_tables(rel_bias_b[l] * LOG2E)
    mask_a = _mask_a_tables()
    bd = _block_diag_ones()
    sel_a, sel_b = _lane_selectors()
    ff1 = (row(g_ff1), bf(w_ff1_gate), bf(w_ff1_up), bf(w_ff1_down), row(g_mix))
    ff2 = (row(g_ff2), bf(w_ff2_gate), bf(w_ff2_up), bf(w_ff2_down), row(g_final))
    merge_w = (bf(w_gate), row(b_gate), w_br_a_bf, bf(w_br_b), bf(w_br_c), bf(w_out))

    def trunk(x, pos, batch_view, attend, keep):
        n = x.shape[0] * x.shape[1]
        xf = x.reshape(n, D_MODEL)
        x1, h = _ffn(xf, *ff1, emit_x=True, norm_dtype=BF16)
        cos, sin = _rope_tables(pos)
        q, ka16, va16, kb16, vb16, *kv32 = _proj(h.reshape(batch_view + (D_MODEL,)), w_in_bf, gain, cos, sin, bd, keep)
        y = attend(q, ka16, va16, kb16, vb16)
        x2 = _merge(h, y.reshape(n, D_MODEL), x1, *merge_w)
        (out,) = _ffn(x2, *ff2, emit_x=False, norm_dtype=F32)
        return out.reshape(x.shape), kv32

    def split_kv(kv32, b, rows):
        kv32 = kv32.reshape(b, rows, KV_COLS)
        ka = kv32[..., :A_KV].reshape(b, rows, A_KV_HEADS, HEAD_DIM)
        va = kv32[..., A_KV:2 * A_KV].reshape(b, rows, A_KV_HEADS, HEAD_DIM)
        kb = kv32[..., 2 * A_KV:2 * A_KV + B_W].reshape(b, rows, B_HEADS, HEAD_DIM)
        vb = kv32[..., 2 * A_KV + B_W:].reshape(b, rows, B_HEADS, HEAD_DIM)
        return ka, va, kb, vb

    g_kc_row = jnp.tile(g_kc[l], C_HEADS).reshape(1, C_W).astype(F32)
    mk32t, mv32t, mk16, mv16 = _memkv(mem_prompt.reshape(bp * N_MEM, D_MODEL), row(g_mem), bf(w_mem_kv), g_kc_row, bd)
    mem_kv = (mk16.reshape(bp, N_MEM, C_W), mv16.reshape(bp, N_MEM, C_W))
    heads_last = lambda a, n_heads: a.reshape(a.shape[0], n_heads, HEAD_DIM, a.shape[-1]).transpose(0, 3, 1, 2)[None]
    mk_p = heads_last(mk32t, C_HEADS)
    mv_p = heads_last(mv32t, C_HEADS)

    q_blocks = (0, A_Q // B_W, (A_Q + B_W) // C_W)

    def attend_prompt(q, ka16, va16, kb16, vb16):
        q_arrays = tuple((q, cb) for cb in q_blocks)
        return _attn(q_arrays, (ka16, va16, kb16, vb16), mem_kv, bias, mask_a, sink, sel_a, sel_b,
                     sq=sp, n_chunks=min(ATTN_CHUNKS_PER_STEP, sp // CHUNK), off_a=0, off_b=0)

    keep = (min(A_PREV_CHUNKS * CHUNK, sp), min(B_PREV_CHUNKS * CHUNK, sp))
    y_p, tails = trunk(x_prompt, jnp.arange(sp, dtype=jnp.int32), (bp, sp), attend_prompt, keep)
    prompt_caches = tuple(heads_last(a, n_heads) for a, n_heads in zip(tails, (A_KV_HEADS, A_KV_HEADS, B_HEADS, B_HEADS)))

    n_s = bs * ss
    rows_per_tile = TOKEN_TILE // ss
    pos_s = PAST_LEN + jnp.tile(jnp.arange(ss, dtype=jnp.int32), rows_per_tile)
    dims_first = lambda c: c[l].transpose(0, 2, 3, 1).reshape(c.shape[1], -1, c.shape[2])
    caches = tuple(dims_first(c) for c in (cache_a_k, cache_a_v, cache_b_k, cache_b_v, cache_mem_k, cache_mem_v))
    assert caches[0].shape[2] == A_KEYS - ss and caches[2].shape[2] <= B_KEYS - ss

    def attend_sample(q, ka16, va16, kb16, vb16):
        per_seq = lambda a: a.reshape(bs, ss, a.shape[-1])
        q_arrays = tuple((per_seq(q), cb) for cb in q_blocks)
        return _attn(q_arrays, tuple(per_seq(a) for a in (ka16, va16, kb16, vb16)), None, bias, mask_a, sink, sel_a, sel_b,
                     sq=ss, n_chunks=1, off_a=(A_KEYS - ss) // CHUNK, off_b=(B_KEYS - ss) // CHUNK, caches=caches)

    y_s, (kv32_s,) = trunk(x_sample, pos_s, (n_s // TOKEN_TILE, TOKEN_TILE), attend_sample, None)
    ka_s, va_s, kb_s, vb_s = split_kv(kv32_s, bs, ss)

    return (y_p, y_s, *prompt_caches, mk_p, mv_p, ka_s[None], va_s[None], kb_s[None], vb_s[None])
```

```python
import functools

import jax
import jax.numpy as jnp
import numpy as np
from jax import lax
from jax.experimental import pallas as pl
from jax.experimental.pallas import tpu as pltpu

D_MODEL = 1024
PAST_LEN = 1024
CHUNK = 64
HEAD_DIM = 64
A_Q_HEADS = 8
A_KV_HEADS = 2
A_GROUP = A_Q_HEADS // A_KV_HEADS
A_PREV_CHUNKS = 2
B_HEADS = 4
B_PREV_CHUNKS = 8
REL_CLIP = 128
C_HEADS = 4
N_MEM = 256
FF_DIM = 2816
ROPE_THETA = 10000.0
EPS = 1e-6
NEG = -1e30

A_Q = A_Q_HEADS * HEAD_DIM
A_KV = A_KV_HEADS * HEAD_DIM
B_W = B_HEADS * HEAD_DIM
C_W = C_HEADS * HEAD_DIM
IN_COLS = A_Q + 2 * A_KV + 3 * B_W + C_W
A_KEYS = (A_PREV_CHUNKS + 1) * CHUNK
B_BAND_CHUNKS = B_PREV_CHUNKS + 2
B_KEYS = B_BAND_CHUNKS * CHUNK
OFF_QA, OFF_KA, OFF_VA = 0, A_Q, A_Q + A_KV
OFF_QB, OFF_KB, OFF_VB = A_Q + 2 * A_KV, A_Q + 2 * A_KV + B_W, A_Q + 2 * A_KV + 2 * B_W
OFF_QC = A_Q + 2 * A_KV + 3 * B_W
KV_COLS = 2 * A_KV + 2 * B_W
A_HEAD_ORDER = (0, 4, 1, 5, 2, 6, 3, 7)

V7X_LANES = 128
V7X_BF16_SUBLANES = 16
V7X_MXU_DIM = 256
V7X_VMEM_LIMIT = 56 * 1024 * 1024

TOKEN_TILE = 1024
FF_CHUNKS = (6 * V7X_MXU_DIM, 5 * V7X_MXU_DIM)
assert sum(FF_CHUNKS) == FF_DIM
FFN_SUBTILES = 4
PROJ_SUBTILES = 4
MERGE_SUBTILES = 4
ATTN_CHUNKS_PER_STEP = 32
ATTN_SLOTS = 4
LOG2E = 1.4426950408889634
BF16 = jnp.bfloat16
F32 = jnp.float32


def _dot(a, b):
    return jnp.dot(a, b, preferred_element_type=F32)


def _dot_nt(a, b):
    return lax.dot_general(a, b, (((1,), (1,)), ((), ())), preferred_element_type=F32)


def _rmsnorm(x, g):
    return x * lax.rsqrt(jnp.mean(x * x, axis=-1, keepdims=True) + EPS) * g


def _sigmoid(z):
    return 1.0 / (1.0 + jnp.exp(-z))


def _const_spec(shape):
    nd = len(shape)
    return pl.BlockSpec(shape, lambda *_: (0,) * nd, pipeline_mode=pl.Buffered(1))


def _params(*sem):
    return pltpu.CompilerParams(dimension_semantics=sem, vmem_limit_bytes=V7X_VMEM_LIMIT)


def _swiglu_residual(x, g_pre, wg_ref, wu_ref, wd_ref):
    hn = _rmsnorm(x, g_pre).astype(BF16)
    acc = None
    lo = 0
    for width in FF_CHUNKS:
        g = _dot(hn, wg_ref[:, lo:lo + width])
        u = _dot(hn, wu_ref[:, lo:lo + width])
        a = (g * _sigmoid(g) * u).astype(BF16)
        d = _dot(a, wd_ref[lo:lo + width, :])
        acc = d if acc is None else acc + d
        lo += width
    return x + 0.5 * acc


def _ffn_kernel(x_ref, gpre_ref, wg_ref, wu_ref, wd_ref, gpost_ref, *refs, emit_x, n_cast):
    cast_in, out_refs, cast_out = refs[:n_cast], refs[n_cast:len(refs) - n_cast], refs[len(refs) - n_cast:]
    n_ref = out_refs[-1]
    sub = TOKEN_TILE // FFN_SUBTILES
    for i in range(FFN_SUBTILES):
        rows = slice(i * sub, (i + 1) * sub)
        y = _swiglu_residual(x_ref[rows, :], gpre_ref[...], wg_ref, wu_ref, wd_ref)
        if emit_x:
            out_refs[0][rows, :] = y
        n_ref[rows, :] = _rmsnorm(y, gpost_ref[...]).astype(n_ref.dtype)
    for i_ref, o_ref in zip(cast_in, cast_out):
        o_ref[...] = i_ref[...].astype(BF16)


def _cast_block_rows(rows, n_steps):
    for r in range(V7X_BF16_SUBLANES, rows + 1, V7X_BF16_SUBLANES):
        if rows % r == 0 and rows // r <= n_steps:
            return r
    raise ValueError((rows, n_steps))


def _ffn(x, g_pre, wg, wu, wd, g_post, *, emit_x, norm_dtype, cast=()):
    n = x.shape[0]
    assert n % TOKEN_TILE == 0, n
    n_steps = n // TOKEN_TILE
    tile = pl.BlockSpec((TOKEN_TILE, D_MODEL), lambda i: (i, 0))
    out_shape = [jax.ShapeDtypeStruct((n, D_MODEL), norm_dtype)]
    out_specs = [tile]
    if emit_x:
        out_shape.insert(0, jax.ShapeDtypeStruct((n, D_MODEL), F32))
        out_specs.insert(0, tile)
    cast_specs = []
    for w in cast:
        r = _cast_block_rows(w.shape[0], n_steps)
        cast_specs.append(pl.BlockSpec((r, w.shape[1]), lambda i, last=w.shape[0] // r - 1: (jnp.minimum(i, last), 0)))
        out_shape.append(jax.ShapeDtypeStruct(w.shape, BF16))
    return pl.pallas_call(
        functools.partial(_ffn_kernel, emit_x=emit_x, n_cast=len(cast)),
        grid=(n_steps,),
        in_specs=[tile, _const_spec((1, D_MODEL)), _const_spec(wg.shape), _const_spec(wu.shape),
                  _const_spec(wd.shape), _const_spec((1, D_MODEL))] + cast_specs,
        out_specs=out_specs + cast_specs,
        out_shape=out_shape,
        compiler_params=_params("arbitrary" if cast else "parallel"),
        name="ffn_x" if emit_x else "ffn_final",
    )(x, g_pre, wg, wu, wd, g_post, *cast)


def _head_inv_rms(y, bd):
    w = y.shape[1]
    ssq = _dot((y * y).astype(BF16), bd[:w, :w])
    return lax.rsqrt(ssq * (1.0 / HEAD_DIM) + EPS)


def _rope(x, cos, sin_signed, first_half):
    rot = jnp.where(first_half, pltpu.roll(x, V7X_LANES - HEAD_DIM // 2, 1), pltpu.roll(x, HEAD_DIM // 2, 1))
    return x * cos + rot * sin_signed


def _proj_kernel(h_ref, w_ref, gain_ref, cos_ref, sin_ref, bd_ref, q_ref, ka_ref, va_ref, kb_ref, vb_ref, *f32_refs,
                 n_tiles, keep):
    bd = bd_ref[...]
    stash_rows = 0 if keep is None else max(keep)
    stash_ref = None if keep is None else f32_refs[-1]
    sub = TOKEN_TILE // PROJ_SUBTILES
    lane = lax.broadcasted_iota(jnp.int32, (sub, V7X_LANES), 1)
    first_half = (lane % HEAD_DIM) < (HEAD_DIM // 2)
    for i in range(PROJ_SUBTILES):
        rows = slice(i * sub, (i + 1) * sub)
        y = _dot(h_ref[0, rows, :], w_ref[...])
        cos = cos_ref[rows, :]
        sin = sin_ref[rows, :]

        def normed(off, width):
            blk = y[:, off:off + width]
            return blk * _head_inv_rms(blk, bd) * gain_ref[:, off:off + width]

        pieces = {}
        for off in range(OFF_QA, OFF_QA + A_Q, V7X_MXU_DIM):
            n = normed(off, V7X_MXU_DIM)
            for s in range(0, V7X_MXU_DIM, V7X_LANES):
                pieces[off + s] = _rope(n[:, s:s + V7X_LANES], cos, sin, first_half)
        pieces[OFF_KA] = _rope(normed(OFF_KA, A_KV), cos, sin, first_half)
        pieces[OFF_VA] = y[:, OFF_VA:OFF_VA + A_KV]
        for off in (OFF_QB, OFF_KB, OFF_QC):
            n = normed(off, V7X_MXU_DIM)
            for s in range(0, V7X_MXU_DIM, V7X_LANES):
                pieces[off + s] = n[:, s:s + V7X_LANES]
        for s in range(0, B_W, V7X_LANES):
            pieces[OFF_VB + s] = y[:, OFF_VB + s:OFF_VB + s + V7X_LANES]

        q_col = 0
        for off, width in ((OFF_QA, A_Q), (OFF_QB, B_W), (OFF_QC, C_W)):
            for s in range(0, width, V7X_LANES):
                q_ref[0, rows, q_col:q_col + V7X_LANES] = pieces[off + s].astype(BF16)
                q_col += V7X_LANES
        tail_lo = max(i * sub, TOKEN_TILE - stash_rows)
        col = 0
        for o_ref, off, width in ((ka_ref, OFF_KA, A_KV), (va_ref, OFF_VA, A_KV), (kb_ref, OFF_KB, B_W), (vb_ref, OFF_VB, B_W)):
            for s in range(0, width, V7X_LANES):
                o_ref[0, rows, s:s + V7X_LANES] = pieces[off + s].astype(BF16)
                if keep is None:
                    f32_refs[0][0, rows, col:col + V7X_LANES] = pieces[off + s]
                elif tail_lo < (i + 1) * sub:
                    stash_ref[tail_lo - (TOKEN_TILE - stash_rows):(i + 1) * sub - (TOKEN_TILE - stash_rows),
                              col:col + V7X_LANES] = pieces[off + s][tail_lo - i * sub:]
                col += V7X_LANES

    if keep is None:
        return

    def write_tails():
        col = 0
        for o_ref, width, n_rows in zip(f32_refs[:4], (A_KV, A_KV, B_W, B_W), (keep[0], keep[0], keep[1], keep[1])):
            for s in range(0, width, V7X_LANES):
                o_ref[0, s:s + V7X_LANES, :] = stash_ref[stash_rows - n_rows:, col:col + V7X_LANES].T
                col += V7X_LANES

    if n_tiles == 1:
        write_tails()
    else:
        pl.when(pl.program_id(1) == n_tiles - 1)(write_tails)


def _proj(h, w_in, gain, cos, sin, bd, keep=None):
    b, s, _ = h.shape
    assert s % TOKEN_TILE == 0, s
    widths = (A_Q + B_W + C_W, A_KV, A_KV, B_W, B_W)
    out_specs = [pl.BlockSpec((1, TOKEN_TILE, w), lambda i, t: (i, t, 0)) for w in widths]
    out_shape = [jax.ShapeDtypeStruct((b, s, w), BF16) for w in widths]
    scratch = []
    if keep is None:
        out_specs.append(pl.BlockSpec((1, TOKEN_TILE, KV_COLS), lambda i, t: (i, 0, 0)))
        out_shape.append(jax.ShapeDtypeStruct((b, TOKEN_TILE, KV_COLS), F32))
    else:
        assert max(keep) <= TOKEN_TILE
        for w, n_rows in zip((A_KV, A_KV, B_W, B_W), (keep[0], keep[0], keep[1], keep[1])):
            out_specs.append(pl.BlockSpec((1, w, n_rows), lambda i, t: (i, 0, 0)))
            out_shape.append(jax.ShapeDtypeStruct((b, w, n_rows), F32))
        scratch.append(pltpu.VMEM((max(keep), KV_COLS), F32))
    return pl.pallas_call(
        functools.partial(_proj_kernel, n_tiles=s // TOKEN_TILE, keep=keep),
        grid=(b, s // TOKEN_TILE),
        in_specs=[pl.BlockSpec((1, TOKEN_TILE, D_MODEL), lambda i, t: (i, t, 0)),
                  _const_spec(w_in.shape), _const_spec(gain.shape),
                  pl.BlockSpec((TOKEN_TILE, V7X_LANES), lambda i, t: (t, 0)),
                  pl.BlockSpec((TOKEN_TILE, V7X_LANES), lambda i, t: (t, 0)),
                  _const_spec(bd.shape)],
        out_specs=out_specs,
        out_shape=out_shape,
        scratch_shapes=scratch,
        compiler_params=_params("parallel", "arbitrary"),
        name="proj",
    )(h, w_in, gain, cos, sin, bd)


def _memkv_kernel(m_ref, gmem_ref, w_ref, gkc_ref, bd_ref, k32t_ref, v32t_ref, k16_ref, v16_ref):
    hn = _rmsnorm(m_ref[...], gmem_ref[...]).astype(BF16)
    y = _dot(hn, w_ref[...])
    k = y[:, :C_W]
    k = k * _head_inv_rms(k, bd_ref[...]) * gkc_ref[...]
    v = y[:, C_W:]
    k16_ref[...] = k.astype(BF16)
    v16_ref[...] = v.astype(BF16)
    for e in range(TOKEN_TILE // N_MEM):
        tok = slice(e * N_MEM, (e + 1) * N_MEM)
        k32t_ref[e] = k[tok].T
        v32t_ref[e] = v[tok].T


def _memkv(mem, g_mem, w, g_kc, bd):
    n = mem.shape[0]
    assert n % TOKEN_TILE == 0 and TOKEN_TILE % N_MEM == 0, n
    seqs = TOKEN_TILE // N_MEM
    return pl.pallas_call(
        _memkv_kernel,
        grid=(n // TOKEN_TILE,),
        in_specs=[pl.BlockSpec((TOKEN_TILE, D_MODEL), lambda i: (i, 0)), _const_spec((1, D_MODEL)),
                  _const_spec(w.shape), _const_spec((1, C_W)), _const_spec(bd.shape)],
        out_specs=[pl.BlockSpec((seqs, C_W, N_MEM), lambda i: (i, 0, 0))] * 2
                  + [pl.BlockSpec((TOKEN_TILE, C_W), lambda i: (i, 0))] * 2,
        out_shape=[jax.ShapeDtypeStruct((n // N_MEM, C_W, N_MEM), F32)] * 2
                  + [jax.ShapeDtypeStruct((n, C_W), BF16)] * 2,
        compiler_params=_params("parallel"),
        name="memkv",
    )(mem, g_mem, w, g_kc, bd)


A_ROWS = A_Q_HEADS * CHUNK
A_SCORE_COLS = 2 * V7X_LANES
B_ROWS = B_HEADS * CHUNK
C_ROWS = C_HEADS * CHUNK


def _attn_kernel(qa_ref, qb_ref, qc_ref, ka_ref, va_ref, kb_ref, vb_ref, mk_ref, mv_ref,
                 bias_ref, maska_ref, sinkpad_ref, sela_ref, selb_ref, y_ref,
                 sa_ref, sb_ref, sc_ref, pa_ref, pb_ref, pc_ref, la_ref, lb_ref, lc_ref,
                 *, n_chunks, off_a, off_b):
    t = pl.program_id(1)
    lane_a = lax.broadcasted_iota(jnp.int32, (CHUNK, V7X_LANES), 1)
    low_half = lane_a < HEAD_DIM
    lane_b = lax.broadcasted_iota(jnp.int32, (CHUNK, B_W), 1) // HEAD_DIM

    def stack_heads(q, sel_ref, n_heads):
        return jnp.concatenate([q * sel_ref[h] for h in range(n_heads)], axis=0)

    def unstack_heads(r, n_heads):
        out = r[(n_heads - 1) * CHUNK:]
        for h in reversed(range(n_heads - 1)):
            out = jnp.where(lane_b == h, r[h * CHUNK:(h + 1) * CHUNK], out)
        return out

    def rows_of(j):
        return pl.ds(pl.multiple_of(j * CHUNK, CHUNK), CHUNK)

    def band(j, off, prev_chunks):
        c = t * n_chunks + j + off
        return pl.multiple_of(jnp.maximum(c - prev_chunks, 0) * CHUNK, CHUNK), jnp.minimum(c, prev_chunks)

    def scores(j, slot, first_chunks):
        rows = rows_of(j)
        start_a, var_a = band(j, off_a, A_PREV_CHUNKS)
        qa = qa_ref[0, rows, :]
        lhs = jnp.concatenate(
            [qa[:, (p // 2) * V7X_LANES:(p // 2 + 1) * V7X_LANES] * sela_ref[p % 2] for p in range(A_Q_HEADS)],
            axis=0)
        s = _dot_nt(lhs, ka_ref[0, pl.ds(start_a, A_KEYS), :])
        if first_chunks:
            s = s + maska_ref[var_a][0:1]
        sa_ref[slot, :, :A_KEYS] = s
        start_b, var_b = band(j, off_b, B_BAND_CHUNKS - 1)
        lhs = stack_heads(qb_ref[0, rows, :], selb_ref, B_HEADS)
        sb_ref[slot] = _dot_nt(lhs, kb_ref[0, pl.ds(start_b, B_KEYS), :]) + bias_ref[var_b]
        lhs = stack_heads(qc_ref[0, rows, :], selb_ref, C_HEADS)
        sc_ref[slot] = _dot_nt(lhs, mk_ref[0])

    def numerators(slot):
        for s_ref, p_ref, l_ref in ((sa_ref, pa_ref, la_ref), (sb_ref, pb_ref, lb_ref), (sc_ref, pc_ref, lc_ref)):
            s = s_ref[slot]
            e = jnp.exp2(s - jnp.max(s, axis=-1, keepdims=True))
            p_ref[slot] = e.astype(BF16)
            l_ref[slot] = jnp.broadcast_to(1.0 / jnp.sum(e, axis=-1, keepdims=True), l_ref.shape[1:])

    def outputs(j, slot):
        rows = rows_of(j)
        start_a, _ = band(j, off_a, A_PREV_CHUNKS)
        r = _dot(pa_ref[slot, :, :A_KEYS], va_ref[0, pl.ds(start_a, A_KEYS), :]) * la_ref[slot]
        for jc in range(A_Q // V7X_LANES):
            ev = r[(2 * jc) * CHUNK:(2 * jc + 1) * CHUNK]
            od = r[(2 * jc + 1) * CHUNK:(2 * jc + 2) * CHUNK]
            y_ref[0, rows, jc * V7X_LANES:(jc + 1) * V7X_LANES] = jnp.where(low_half, ev, od).astype(BF16)
        start_b, _ = band(j, off_b, B_BAND_CHUNKS - 1)
        inv_l = lb_ref[slot]
        r = _dot(pb_ref[slot], vb_ref[0, pl.ds(start_b, B_KEYS), :]) * jnp.concatenate([inv_l, inv_l], axis=1)
        y_ref[0, rows, A_Q:A_Q + B_W] = unstack_heads(r, B_HEADS).astype(BF16)
        inv_l = lc_ref[slot]
        r = _dot(pc_ref[slot], mv_ref[0]) * jnp.concatenate([inv_l, inv_l], axis=1)
        y_ref[0, rows, A_Q + B_W:] = unstack_heads(r, C_HEADS).astype(BF16)

    for slot in range(ATTN_SLOTS):
        sa_ref[slot, :, A_KEYS:] = sinkpad_ref[...]

    if n_chunks < ATTN_SLOTS:
        for j in range(n_chunks):
            scores(j, 0, j + off_a < A_PREV_CHUNKS)
            numerators(0)
            outputs(j, 0)
        return

    assert A_PREV_CHUNKS <= 2

    def step(j, slot):
        scores(j, slot, False)
        numerators((slot - 1) % ATTN_SLOTS)
        outputs(j - 2, (slot - 2) % ATTN_SLOTS)

    scores(0, 0, True)
    scores(1, 1, True)
    numerators(0)
    first = 2
    n_groups = (n_chunks - first) // ATTN_SLOTS

    def group(g, carry):
        for k in range(ATTN_SLOTS):
            step(first + g * ATTN_SLOTS + k, (first + k) % ATTN_SLOTS)
        return carry

    lax.fori_loop(0, n_groups, group, 0)
    for j in range(first + n_groups * ATTN_SLOTS, n_chunks):
        step(j, j % ATTN_SLOTS)
    numerators((n_chunks - 1) % ATTN_SLOTS)
    outputs(n_chunks - 2, (n_chunks - 2) % ATTN_SLOTS)
    outputs(n_chunks - 1, (n_chunks - 1) % ATTN_SLOTS)


def _attn_cached_kernel(qa_ref, qb_ref, qc_ref, ka_new, va_new, kb_new, vb_new, cak_ref, cav_ref, cbk_ref, cbv_ref,
                        cmk_ref, cmv_ref, bias_ref, maska_ref, sinkpad_ref, sela_ref, selb_ref, y_ref, *scratch,
                        off_a, off_b):
    *stage_scratch, ka_s, va_s, kb_s, vb_s, mk_s, mv_s = scratch
    for band_ref, cache_ref, new_ref in ((ka_s, cak_ref, ka_new), (va_s, cav_ref, va_new),
                                         (kb_s, cbk_ref, kb_new), (vb_s, cbv_ref, vb_new)):
        n_hist, n_new = cache_ref.shape[2], new_ref.shape[1]
        lead = band_ref.shape[1] - n_hist - n_new
        if lead:
            band_ref[0, :lead, :] = jnp.zeros((lead, band_ref.shape[2]), BF16)
        band_ref[0, lead:lead + n_hist, :] = cache_ref[0].T.astype(BF16)
        band_ref[0, lead + n_hist:, :] = new_ref[0]
    mk_s[0] = cmk_ref[0].T.astype(BF16)
    mv_s[0] = cmv_ref[0].T.astype(BF16)
    _attn_kernel(qa_ref, qb_ref, qc_ref, ka_s, va_s, kb_s, vb_s, mk_s, mv_s, bias_ref, maska_ref, sinkpad_ref,
                 sela_ref, selb_ref, y_ref, *stage_scratch, n_chunks=1, off_a=off_a, off_b=off_b)


def _attn(q_arrays, kv_arrays, mem_kv, bias, mask_a, sink, sel_a, sel_b, *, sq, n_chunks, off_a, off_b, caches=None):
    b = q_arrays[0][0].shape[0]
    tq = n_chunks * CHUNK
    in_specs, args = [], []
    for (arr, cb), w in zip(q_arrays, (A_Q, B_W, C_W)):
        in_specs.append(pl.BlockSpec((1, tq, w), lambda i, t, cb=cb: (i, t, cb)))
        args.append(arr)
    for arr in tuple(kv_arrays) + tuple(mem_kv if caches is None else caches):
        in_specs.append(pl.BlockSpec((1,) + arr.shape[1:], lambda i, t: (i, 0, 0)))
        args.append(arr)
    for arr in (bias, mask_a, sink, sel_a, sel_b):
        in_specs.append(_const_spec(arr.shape))
        args.append(arr)
    scratch = [pltpu.VMEM((ATTN_SLOTS, rows, cols), dtype)
               for dtype, cols_of in ((F32, (A_SCORE_COLS, B_KEYS, N_MEM)), (BF16, (A_SCORE_COLS, B_KEYS, N_MEM)),
                                      (F32, (V7X_LANES,) * 3))
               for rows, cols in zip((A_ROWS, B_ROWS, C_ROWS), cols_of)]
    if caches is None:
        body = functools.partial(_attn_kernel, n_chunks=n_chunks, off_a=off_a, off_b=off_b)
    else:
        assert n_chunks == 1
        body = functools.partial(_attn_cached_kernel, off_a=off_a, off_b=off_b)
        scratch += [pltpu.VMEM((1, rows, cols), BF16)
                    for rows, cols in ((A_KEYS, A_KV), (A_KEYS, A_KV), (B_KEYS, B_W), (B_KEYS, B_W), (N_MEM, C_W), (N_MEM, C_W))]
    return pl.pallas_call(
        body,
        grid=(b, sq // tq),
        in_specs=in_specs,
        out_specs=pl.BlockSpec((1, tq, D_MODEL), lambda i, t: (i, t, 0)),
        out_shape=jax.ShapeDtypeStruct((b, sq, D_MODEL), BF16),
        scratch_shapes=scratch,
        compiler_params=_params("parallel", "arbitrary"),
        name="attn",
    )(*args)


def _merge_kernel(h_ref, y_ref, x_ref, wgate_ref, bgate_ref, wa_ref, wb_ref, wc_ref, wout_ref, o_ref):
    sub = TOKEN_TILE // MERGE_SUBTILES
    for t in range(MERGE_SUBTILES):
        rows = slice(t * sub, (t + 1) * sub)
        h = h_ref[rows, :]
        merged = None
        col = 0
        for i, (w_ref, width) in enumerate(((wa_ref, A_Q), (wb_ref, B_W), (wc_ref, C_W))):
            gate = _sigmoid(_dot(h, wgate_ref[:, i * D_MODEL:(i + 1) * D_MODEL])
                            + bgate_ref[:, i * D_MODEL:(i + 1) * D_MODEL])
            term = gate * _dot(y_ref[rows, col:col + width], w_ref[...])
            merged = term if merged is None else merged + term
            col += width
        o_ref[rows, :] = x_ref[rows, :] + _dot(merged.astype(BF16), wout_ref[...])


def _merge(h, y, x, w_gate, b_gate, wa, wb, wc, w_out):
    n = h.shape[0]
    assert n % TOKEN_TILE == 0, n
    tile = pl.BlockSpec((TOKEN_TILE, D_MODEL), lambda i: (i, 0))
    consts = (w_gate, b_gate, wa, wb, wc, w_out)
    return pl.pallas_call(
        _merge_kernel,
        grid=(n // TOKEN_TILE,),
        in_specs=[tile, tile, tile] + [_const_spec(c.shape) for c in consts],
        out_specs=tile,
        out_shape=jax.ShapeDtypeStruct((n, D_MODEL), F32),
        compiler_params=_params("parallel"),
        name="merge",
    )(h, y, x, *consts)


def _rope_tables(pos):
    half = HEAD_DIM // 2
    inv = ROPE_THETA ** (-jnp.arange(half, dtype=F32) / half)
    ang = pos.astype(F32)[:, None] * inv[None, :]
    cos, sin = jnp.cos(ang), jnp.sin(ang)
    reps = V7X_LANES // HEAD_DIM
    return jnp.tile(jnp.concatenate([cos, cos], -1), (1, reps)), jnp.tile(jnp.concatenate([-sin, sin], -1), (1, reps))


def _bias_tables(rel_bias):
    rel_bias = rel_bias.astype(F32)
    near_lags = (REL_CLIP + CHUNK - 1) // CHUNK + 1
    ext = jnp.pad(rel_bias, ((0, 0), (0, near_lags * CHUNK - REL_CLIP)), mode="edge")
    period = 2 * CHUNK
    table = jnp.broadcast_to(rel_bias[:, None, -1:], (B_HEADS, CHUNK, B_KEYS))
    lag = np.arange(B_BAND_CHUNKS)[:, None, None, None] - (np.arange(B_KEYS) // CHUNK)[None, None, None, :]
    table = jnp.broadcast_to(table[None], (B_BAND_CHUNKS,) + table.shape)
    for g in range(near_lags):
        seg = ext[:, g * CHUNK + REL_CLIP - CHUNK + 1:g * CHUNK + REL_CLIP + CHUNK]
        ring = jnp.concatenate([seg[:, CHUNK - 1::-1], jnp.zeros((B_HEADS, 1), F32), seg[:, :CHUNK - 1:-1]], axis=1)
        block = jnp.tile(ring, (1, CHUNK))[:, :CHUNK * (period - 1)].reshape(B_HEADS, CHUNK, period - 1)[:, :, :CHUNK]
        table = jnp.where(lag == g, jnp.tile(block, (1, 1, B_BAND_CHUNKS))[None], table)
    visible = (lag >= 0) & (lag <= B_PREV_CHUNKS)
    return jnp.where(visible, table, NEG).reshape(B_BAND_CHUNKS, B_HEADS * CHUNK, B_KEYS)


def _mask_a_tables():
    j = np.arange(A_KEYS)[None, :]
    rows = [np.where((j // CHUNK) <= v, 0.0, NEG) for v in range(A_PREV_CHUNKS + 1)]
    return jnp.asarray(np.stack([np.broadcast_to(r, (8, A_KEYS)) for r in rows]), F32)


def _block_diag_ones():
    i = np.arange(V7X_MXU_DIM)
    return jnp.asarray(i[:, None] // HEAD_DIM == i[None, :] // HEAD_DIM, BF16)


def _lane_selectors():
    lane = np.arange(V7X_LANES)[None, :] // HEAD_DIM
    sel_a = np.stack([np.broadcast_to(lane == p, (CHUNK, V7X_LANES)) for p in range(2)])
    lane = np.arange(B_W)[None, :] // HEAD_DIM
    sel_b = np.stack([np.broadcast_to(lane == h, (CHUNK, B_W)) for h in range(B_HEADS)])
    return jnp.asarray(sel_a, BF16), jnp.asarray(sel_b, BF16)


def kernel(x_prompt, x_sample, cache_a_k, cache_a_v, cache_b_k, cache_b_v, cache_mem_k, cache_mem_v,
           mem_prompt, g_ff1, w_ff1_gate, w_ff1_up, w_ff1_down, g_mix, w_in, g_qa, g_ka, sinks_a,
           g_qb, g_kb, rel_bias_b, g_qc, g_mem, w_mem_kv, g_kc, w_gate, b_gate, w_br_a, w_br_b,
           w_br_c, w_out, g_ff2, w_ff2_gate, w_ff2_up, w_ff2_down, g_final):
    bp, sp, _ = x_prompt.shape
    bs, ss, _ = x_sample.shape
    l = 0
    row = lambda g: g[l].reshape(1, -1).astype(F32)
    bf = lambda w: w[l].astype(BF16)

    order = np.array(A_HEAD_ORDER)
    head_cols = (order[:, None] * HEAD_DIM + np.arange(HEAD_DIM)[None, :]).reshape(-1)
    w_in_l = bf(w_in)
    w_in_bf = jnp.concatenate([w_in_l[:, :A_Q][:, head_cols], w_in_l[:, A_Q:]], axis=1)
    w_br_a_bf = bf(w_br_a)[head_cols]
    scale = HEAD_DIM ** -0.5 * LOG2E
    ones = lambda n: jnp.ones((n,), F32)
    gain = jnp.concatenate([
        jnp.tile(g_qa[l], A_Q_HEADS) * scale, jnp.tile(g_ka[l], A_KV_HEADS), ones(A_KV),
        jnp.tile(g_qb[l], B_HEADS) * scale, jnp.tile(g_kb[l], B_HEADS), ones(B_W),
        jnp.tile(g_qc[l], C_HEADS) * scale]).reshape(1, IN_COLS).astype(F32)
    sink = jnp.concatenate([jnp.repeat(sinks_a[l][order].astype(F32) * LOG2E, CHUNK)[:, None],
                            jnp.full((A_ROWS, A_SCORE_COLS - A_KEYS - 1), NEG, F32)], axis=1)
    bias = _bias_tables(rel_bias_b[l] * LOG2E)
    mask_a = _mask_a_tables()
    bd = _block_diag_ones()
    sel_a, sel_b = _lane_selectors()
    ff1 = (row(g_ff1), bf(w_ff1_gate), bf(w_ff1_up), bf(w_ff1_down), row(g_mix))
    late_f32 = (w_ff2_gate[l], w_ff2_up[l], w_ff2_down[l], w_gate[l], w_out[l])

    def trunk(x, pos, batch_view, attend, keep, late_bf16):
        n = x.shape[0] * x.shape[1]
        xf = x.reshape(n, D_MODEL)
        if late_bf16 is None:
            x1, h, *late_bf16 = _ffn(xf, *ff1, emit_x=True, norm_dtype=BF16, cast=late_f32)
        else:
            x1, h = _ffn(xf, *ff1, emit_x=True, norm_dtype=BF16)
        w2_gate, w2_up, w2_down, w_gate_bf, w_out_bf = late_bf16
        cos, sin = _rope_tables(pos)
        q, ka16, va16, kb16, vb16, *kv32 = _proj(h.reshape(batch_view + (D_MODEL,)), w_in_bf, gain, cos, sin, bd, keep)
        y = attend(q, ka16, va16, kb16, vb16)
        x2 = _merge(h, y.reshape(n, D_MODEL), x1, w_gate_bf, row(b_gate), w_br_a_bf, bf(w_br_b), bf(w_br_c), w_out_bf)
        (out,) = _ffn(x2, row(g_ff2), w2_gate, w2_up, w2_down, row(g_final), emit_x=False, norm_dtype=F32)
        return out.reshape(x.shape), kv32, late_bf16

    def split_kv(kv32, b, rows):
        kv32 = kv32.reshape(b, rows, KV_COLS)
        ka = kv32[..., :A_KV].reshape(b, rows, A_KV_HEADS, HEAD_DIM)
        va = kv32[..., A_KV:2 * A_KV].reshape(b, rows, A_KV_HEADS, HEAD_DIM)
        kb = kv32[..., 2 * A_KV:2 * A_KV + B_W].reshape(b, rows, B_HEADS, HEAD_DIM)
        vb = kv32[..., 2 * A_KV + B_W:].reshape(b, rows, B_HEADS, HEAD_DIM)
        return ka, va, kb, vb

    g_kc_row = jnp.tile(g_kc[l], C_HEADS).reshape(1, C_W).astype(F32)
    mk32t, mv32t, mk16, mv16 = _memkv(mem_prompt.reshape(bp * N_MEM, D_MODEL), row(g_mem), bf(w_mem_kv), g_kc_row, bd)
    mem_kv = (mk16.reshape(bp, N_MEM, C_W), mv16.reshape(bp, N_MEM, C_W))
    heads_last = lambda a, n_heads: a.reshape(a.shape[0], n_heads, HEAD_DIM, a.shape[-1]).transpose(0, 3, 1, 2)[None]
    mk_p = heads_last(mk32t, C_HEADS)
    mv_p = heads_last(mv32t, C_HEADS)

    q_blocks = (0, A_Q // B_W, (A_Q + B_W) // C_W)

    def attend_prompt(q, ka16, va16, kb16, vb16):
        q_arrays = tuple((q, cb) for cb in q_blocks)
        return _attn(q_arrays, (ka16, va16, kb16, vb16), mem_kv, bias, mask_a, sink, sel_a, sel_b,
                     sq=sp, n_chunks=min(ATTN_CHUNKS_PER_STEP, sp // CHUNK), off_a=0, off_b=0)

    keep = (min(A_PREV_CHUNKS * CHUNK, sp), min(B_PREV_CHUNKS * CHUNK, sp))
    y_p, tails, late_bf16 = trunk(x_prompt, jnp.arange(sp, dtype=jnp.int32), (bp, sp), attend_prompt, keep, None)
    prompt_caches = tuple(heads_last(a, n_heads) for a, n_heads in zip(tails, (A_KV_HEADS, A_KV_HEADS, B_HEADS, B_HEADS)))

    n_s = bs * ss
    rows_per_tile = TOKEN_TILE // ss
    pos_s = PAST_LEN + jnp.tile(jnp.arange(ss, dtype=jnp.int32), rows_per_tile)
    dims_first = lambda c: c[l].transpose(0, 2, 3, 1).reshape(c.shape[1], -1, c.shape[2])
    caches = tuple(dims_first(c) for c in (cache_a_k, cache_a_v, cache_b_k, cache_b_v, cache_mem_k, cache_mem_v))
    assert caches[0].shape[2] == A_KEYS - ss and caches[2].shape[2] <= B_KEYS - ss

    def attend_sample(q, ka16, va16, kb16, vb16):
        per_seq = lambda a: a.reshape(bs, ss, a.shape[-1])
        q_arrays = tuple((per_seq(q), cb) for cb in q_blocks)
        return _attn(q_arrays, tuple(per_seq(a) for a in (ka16, va16, kb16, vb16)), None, bias, mask_a, sink, sel_a, sel_b,
                     sq=ss, n_chunks=1, off_a=(A_KEYS - ss) // CHUNK, off_b=(B_KEYS - ss) // CHUNK, caches=caches)

    y_s, (kv32_s,), _ = trunk(x_sample, pos_s, (n_s // TOKEN_TILE, TOKEN_TILE), attend_sample, None, late_bf16)
    ka_s, va_s, kb_s, vb_s = split_kv(kv32_s, bs, ss)

    return (y_p, y_s, *prompt_caches, mk_p, mv_p, ka_s[None], va_s[None], kb_s[None], vb_s[None])
```

```python
import functools

import jax
import jax.numpy as jnp
from jax import lax
from jax.experimental import pallas as pl
from jax.experimental.pallas import tpu as pltpu

D_MODEL = 1024
PAST_LEN = 1024
CHUNK = 64
HEAD_DIM = 64
A_Q_HEADS = 8
A_KV_HEADS = 2
A_GROUP = A_Q_HEADS // A_KV_HEADS
A_PREV_CHUNKS = 2
B_HEADS = 4
B_PREV_CHUNKS = 8
REL_CLIP = 128
C_HEADS = 4
N_MEM = 256
FF_DIM = 2816
ROPE_THETA = 10000.0
EPS = 1e-6
NEG = -1e30

A_Q = A_Q_HEADS * HEAD_DIM
A_KV = A_KV_HEADS * HEAD_DIM
B_W = B_HEADS * HEAD_DIM
C_W = C_HEADS * HEAD_DIM
IN_COLS = A_Q + 2 * A_KV + 3 * B_W + C_W
A_KEYS = (A_PREV_CHUNKS + 1) * CHUNK
B_BAND_CHUNKS = B_PREV_CHUNKS + 2
B_KEYS = B_BAND_CHUNKS * CHUNK
OFF_QA, OFF_KA, OFF_VA = 0, A_Q, A_Q + A_KV
OFF_QB, OFF_KB, OFF_VB = A_Q + 2 * A_KV, A_Q + 2 * A_KV + B_W, A_Q + 2 * A_KV + 2 * B_W
OFF_QC = A_Q + 2 * A_KV + 3 * B_W
KV_COLS = 2 * A_KV + 2 * B_W
A_HEAD_ORDER = (0, 4, 1, 5, 2, 6, 3, 7)

V7X_LANES = 128
V7X_BF16_SUBLANES = 16
V7X_MXU_DIM = 256
V7X_VMEM_LIMIT = 56 * 1024 * 1024

TOKEN_TILE = 1024
FF_CHUNKS = (6 * V7X_MXU_DIM, 5 * V7X_MXU_DIM)
assert sum(FF_CHUNKS) == FF_DIM
FF_STREAM_CHUNK = V7X_MXU_DIM
FFN_SUBTILES = 4
PROJ_SUBTILES = 4
MERGE_SUBTILES = 4
ATTN_CHUNKS_PER_STEP = 32
ATTN_SLOTS = 4
LOG2E = 1.4426950408889634
BF16 = jnp.bfloat16
F32 = jnp.float32


def _dot(a, b):
    return jnp.dot(a, b, preferred_element_type=F32)


def _dot_nt(a, b):
    return lax.dot_general(a, b, (((1,), (1,)), ((), ())), preferred_element_type=F32)


def _rmsnorm(x, g):
    return x * lax.rsqrt(jnp.mean(x * x, axis=-1, keepdims=True) + EPS) * g


def _sigmoid(z):
    return 1.0 / (1.0 + jnp.exp(-z))


def _const_spec(shape):
    nd = len(shape)
    return pl.BlockSpec(shape, lambda *_: (0,) * nd, pipeline_mode=pl.Buffered(1))


def _params(*sem):
    return pltpu.CompilerParams(dimension_semantics=sem, vmem_limit_bytes=V7X_VMEM_LIMIT)


def _swiglu_residual(x, g_pre, wg_ref, wu_ref, wd_ref):
    hn = _rmsnorm(x, g_pre).astype(BF16)
    acc = None
    lo = 0
    for width in FF_CHUNKS:
        g = _dot(hn, wg_ref[:, lo:lo + width])
        u = _dot(hn, wu_ref[:, lo:lo + width])
        a = (g * _sigmoid(g) * u).astype(BF16)
        d = _dot(a, wd_ref[lo:lo + width, :])
        acc = d if acc is None else acc + d
        lo += width
    return x + 0.5 * acc


def _ffn_kernel(x_ref, gpre_ref, wg_ref, wu_ref, wd_ref, gpost_ref, *refs, emit_x, n_cast):
    cast_in, out_refs, cast_out = refs[:n_cast], refs[n_cast:len(refs) - n_cast], refs[len(refs) - n_cast:]
    n_ref = out_refs[-1]
    sub = TOKEN_TILE // FFN_SUBTILES
    for i in range(FFN_SUBTILES):
        rows = slice(i * sub, (i + 1) * sub)
        y = _swiglu_residual(x_ref[rows, :], gpre_ref[...], wg_ref, wu_ref, wd_ref)
        if emit_x:
            out_refs[0][rows, :] = y
        n_ref[rows, :] = _rmsnorm(y, gpost_ref[...]).astype(n_ref.dtype)
    for i_ref, o_ref in zip(cast_in, cast_out):
        o_ref[...] = i_ref[...].astype(BF16)


def _cast_block_rows(rows, n_steps):
    for r in range(V7X_BF16_SUBLANES, rows + 1, V7X_BF16_SUBLANES):
        if rows % r == 0 and rows // r <= n_steps:
            return r
    raise ValueError((rows, n_steps))


def _ffn(x, g_pre, wg, wu, wd, g_post, *, emit_x, norm_dtype, cast=()):
    n = x.shape[0]
    assert n % TOKEN_TILE == 0, n
    n_steps = n // TOKEN_TILE
    tile = pl.BlockSpec((TOKEN_TILE, D_MODEL), lambda i: (i, 0))
    out_shape = [jax.ShapeDtypeStruct((n, D_MODEL), norm_dtype)]
    out_specs = [tile]
    if emit_x:
        out_shape.insert(0, jax.ShapeDtypeStruct((n, D_MODEL), F32))
        out_specs.insert(0, tile)
    cast_specs = []
    for w in cast:
        r = _cast_block_rows(w.shape[0], n_steps)
        cast_specs.append(pl.BlockSpec((r, w.shape[1]), lambda i, last=w.shape[0] // r - 1: (jnp.minimum(i, last), 0)))
        out_shape.append(jax.ShapeDtypeStruct(w.shape, BF16))
    return pl.pallas_call(
        functools.partial(_ffn_kernel, emit_x=emit_x, n_cast=len(cast)),
        grid=(n_steps,),
        in_specs=[tile, _const_spec((1, D_MODEL)), _const_spec(wg.shape), _const_spec(wu.shape),
                  _const_spec(wd.shape), _const_spec((1, D_MODEL))] + cast_specs,
        out_specs=out_specs + cast_specs,
        out_shape=out_shape,
        compiler_params=_params("arbitrary" if cast else "parallel"),
        name="ffn_x" if emit_x else "ffn_final",
    )(x, g_pre, wg, wu, wd, g_post, *cast)


def _ffn_stream_kernel(x_ref, gpre_ref, wg_ref, wu_ref, wd_ref, gpost_ref, *refs, emit_x):
    *out_refs, hn_ref, acc_ref = refs
    c = pl.program_id(0)

    @pl.when(c == 0)
    def _():
        hn_ref[...] = _rmsnorm(x_ref[...], gpre_ref[...]).astype(BF16)
        acc_ref[...] = jnp.zeros_like(acc_ref)

    hn = hn_ref[...]
    g = _dot(hn, wg_ref[...])
    u = _dot(hn, wu_ref[...])
    acc_ref[...] += _dot((g * _sigmoid(g) * u).astype(BF16), wd_ref[...])

    @pl.when(c == pl.num_programs(0) - 1)
    def _():
        y = x_ref[...] + 0.5 * acc_ref[...]
        if emit_x:
            out_refs[0][...] = y
        out_refs[-1][...] = _rmsnorm(y, gpost_ref[...]).astype(out_refs[-1].dtype)


def _ffn_stream(x, g_pre, wg, wu, wd, g_post, *, emit_x, norm_dtype):
    n = x.shape[0]
    assert n == TOKEN_TILE and FF_DIM % FF_STREAM_CHUNK == 0
    whole = pl.BlockSpec((n, D_MODEL), lambda c: (0, 0))
    out_shape = [jax.ShapeDtypeStruct((n, D_MODEL), norm_dtype)]
    if emit_x:
        out_shape.insert(0, jax.ShapeDtypeStruct((n, D_MODEL), F32))
    return pl.pallas_call(
        functools.partial(_ffn_stream_kernel, emit_x=emit_x),
        grid=(FF_DIM // FF_STREAM_CHUNK,),
        in_specs=[whole, _const_spec((1, D_MODEL)),
                  pl.BlockSpec((D_MODEL, FF_STREAM_CHUNK), lambda c: (0, c)),
                  pl.BlockSpec((D_MODEL, FF_STREAM_CHUNK), lambda c: (0, c)),
                  pl.BlockSpec((FF_STREAM_CHUNK, D_MODEL), lambda c: (c, 0)), _const_spec((1, D_MODEL))],
        out_specs=[whole] * len(out_shape),
        out_shape=out_shape,
        scratch_shapes=[pltpu.VMEM((n, D_MODEL), BF16), pltpu.VMEM((n, D_MODEL), F32)],
        compiler_params=_params("arbitrary"),
        name="ffn_stream_x" if emit_x else "ffn_stream_final",
    )(x, g_pre, wg, wu, wd, g_post)


def _head_inv_rms(y, bd):
    w = y.shape[1]
    ssq = _dot((y * y).astype(BF16), bd[:w, :w])
    return lax.rsqrt(ssq * (1.0 / HEAD_DIM) + EPS)


def _rope(x, cos, sin_signed, first_half):
    rot = jnp.where(first_half, pltpu.roll(x, V7X_LANES - HEAD_DIM // 2, 1), pltpu.roll(x, HEAD_DIM // 2, 1))
    return x * cos + rot * sin_signed


def _proj_kernel(h_ref, w_ref, gain_ref, cos_ref, sin_ref, bd_ref, q_ref, ka_ref, va_ref, kb_ref, vb_ref, *f32_refs,
                 n_tiles, keep):
    bd = bd_ref[...]
    stash_rows = 0 if keep is None else max(keep)
    stash_ref = None if keep is None else f32_refs[-1]
    sub = TOKEN_TILE // PROJ_SUBTILES
    lane = lax.broadcasted_iota(jnp.int32, (sub, V7X_LANES), 1)
    first_half = (lane % HEAD_DIM) < (HEAD_DIM // 2)
    for i in range(PROJ_SUBTILES):
        rows = slice(i * sub, (i + 1) * sub)
        y = _dot(h_ref[0, rows, :], w_ref[...])
        cos = cos_ref[rows, :]
        sin = sin_ref[rows, :]

        def normed(off, width):
            blk = y[:, off:off + width]
            return blk * _head_inv_rms(blk, bd) * gain_ref[:, off:off + width]

        pieces = {}
        for off in range(OFF_QA, OFF_QA + A_Q, V7X_MXU_DIM):
            n = normed(off, V7X_MXU_DIM)
            for s in range(0, V7X_MXU_DIM, V7X_LANES):
                pieces[off + s] = _rope(n[:, s:s + V7X_LANES], cos, sin, first_half)
        pieces[OFF_KA] = _rope(normed(OFF_KA, A_KV), cos, sin, first_half)
        pieces[OFF_VA] = y[:, OFF_VA:OFF_VA + A_KV]
        for off in (OFF_QB, OFF_KB, OFF_QC):
            n = normed(off, V7X_MXU_DIM)
            for s in range(0, V7X_MXU_DIM, V7X_LANES):
                pieces[off + s] = n[:, s:s + V7X_LANES]
        for s in range(0, B_W, V7X_LANES):
            pieces[OFF_VB + s] = y[:, OFF_VB + s:OFF_VB + s + V7X_LANES]

        q_col = 0
        for off, width in ((OFF_QA, A_Q), (OFF_QB, B_W), (OFF_QC, C_W)):
            for s in range(0, width, V7X_LANES):
                q_ref[0, rows, q_col:q_col + V7X_LANES] = pieces[off + s].astype(BF16)
                q_col += V7X_LANES
        tail_lo = max(i * sub, TOKEN_TILE - stash_rows)
        col = 0
        for o_ref, off, width in ((ka_ref, OFF_KA, A_KV), (va_ref, OFF_VA, A_KV), (kb_ref, OFF_KB, B_W), (vb_ref, OFF_VB, B_W)):
            for s in range(0, width, V7X_LANES):
                o_ref[0, rows, s:s + V7X_LANES] = pieces[off + s].astype(BF16)
                if keep is None:
                    f32_refs[0][0, rows, col:col + V7X_LANES] = pieces[off + s]
                elif tail_lo < (i + 1) * sub:
                    stash_ref[tail_lo - (TOKEN_TILE - stash_rows):(i + 1) * sub - (TOKEN_TILE - stash_rows),
                              col:col + V7X_LANES] = pieces[off + s][tail_lo - i * sub:]
                col += V7X_LANES

    if keep is None:
        return

    def write_tails():
        col = 0
        for o_ref, width, n_rows in zip(f32_refs[:4], (A_KV, A_KV, B_W, B_W), (keep[0], keep[0], keep[1], keep[1])):
            for s in range(0, width, V7X_LANES):
                o_ref[0, s:s + V7X_LANES, :] = stash_ref[stash_rows - n_rows:, col:col + V7X_LANES].T
                col += V7X_LANES

    if n_tiles == 1:
        write_tails()
    else:
        pl.when(pl.program_id(1) == n_tiles - 1)(write_tails)


def _proj(h, w_in, gain, cos, sin, bd, keep=None):
    b, s, _ = h.shape
    assert s % TOKEN_TILE == 0, s
    widths = (A_Q + B_W + C_W, A_KV, A_KV, B_W, B_W)
    out_specs = [pl.BlockSpec((1, TOKEN_TILE, w), lambda i, t: (i, t, 0)) for w in widths]
    out_shape = [jax.ShapeDtypeStruct((b, s, w), BF16) for w in widths]
    scratch = []
    if keep is None:
        out_specs.append(pl.BlockSpec((1, TOKEN_TILE, KV_COLS), lambda i, t: (i, 0, 0)))
        out_shape.append(jax.ShapeDtypeStruct((b, TOKEN_TILE, KV_COLS), F32))
    else:
        assert max(keep) <= TOKEN_TILE
        for w, n_rows in zip((A_KV, A_KV, B_W, B_W), (keep[0], keep[0], keep[1], keep[1])):
            out_specs.append(pl.BlockSpec((1, w, n_rows), lambda i, t: (i, 0, 0)))
            out_shape.append(jax.ShapeDtypeStruct((b, w, n_rows), F32))
        scratch.append(pltpu.VMEM((max(keep), KV_COLS), F32))
    return pl.pallas_call(
        functools.partial(_proj_kernel, n_tiles=s // TOKEN_TILE, keep=keep),
        grid=(b, s // TOKEN_TILE),
        in_specs=[pl.BlockSpec((1, TOKEN_TILE, D_MODEL), lambda i, t: (i, t, 0)),
                  _const_spec(w_in.shape), _const_spec(gain.shape),
                  pl.BlockSpec((TOKEN_TILE, V7X_LANES), lambda i, t: (t, 0)),
                  pl.BlockSpec((TOKEN_TILE, V7X_LANES), lambda i, t: (t, 0)),
                  _const_spec(bd.shape)],
        out_specs=out_specs,
        out_shape=out_shape,
        scratch_shapes=scratch,
        compiler_params=_params("parallel", "arbitrary"),
        name="proj",
    )(h, w_in, gain, cos, sin, bd)


def _memkv_kernel(m_ref, gmem_ref, w_ref, gkc_ref, bd_ref, k32t_ref, v32t_ref, k16_ref, v16_ref):
    hn = _rmsnorm(m_ref[...], gmem_ref[...]).astype(BF16)
    y = _dot(hn, w_ref[...])
    k = y[:, :C_W]
    k = k * _head_inv_rms(k, bd_ref[...]) * gkc_ref[...]
    v = y[:, C_W:]
    k16_ref[...] = k.astype(BF16)
    v16_ref[...] = v.astype(BF16)
    for e in range(TOKEN_TILE // N_MEM):
        tok = slice(e * N_MEM, (e + 1) * N_MEM)
        k32t_ref[e] = k[tok].T
        v32t_ref[e] = v[tok].T


def _memkv(mem, g_mem, w, g_kc, bd):
    n = mem.shape[0]
    assert n % TOKEN_TILE == 0 and TOKEN_TILE % N_MEM == 0, n
    seqs = TOKEN_TILE // N_MEM
    return pl.pallas_call(
        _memkv_kernel,
        grid=(n // TOKEN_TILE,),
        in_specs=[pl.BlockSpec((TOKEN_TILE, D_MODEL), lambda i: (i, 0)), _const_spec((1, D_MODEL)),
                  _const_spec(w.shape), _const_spec((1, C_W)), _const_spec(bd.shape)],
        out_specs=[pl.BlockSpec((seqs, C_W, N_MEM), lambda i: (i, 0, 0))] * 2
                  + [pl.BlockSpec((TOKEN_TILE, C_W), lambda i: (i, 0))] * 2,
        out_shape=[jax.ShapeDtypeStruct((n // N_MEM, C_W, N_MEM), F32)] * 2
                  + [jax.ShapeDtypeStruct((n, C_W), BF16)] * 2,
        compiler_params=_params("parallel"),
        name="memkv",
    )(mem, g_mem, w, g_kc, bd)


A_ROWS = A_Q_HEADS * CHUNK
A_SCORE_COLS = 2 * V7X_LANES
B_ROWS = B_HEADS * CHUNK
C_ROWS = C_HEADS * CHUNK


def _attn_kernel(qa_ref, qb_ref, qc_ref, ka_ref, va_ref, kb_ref, vb_ref, mk_ref, mv_ref,
                 bias_ref, maska_ref, sinkpad_ref, sela_ref, selb_ref, y_ref,
                 sa_ref, sb_ref, sc_ref, pa_ref, pb_ref, pc_ref, la_ref, lb_ref, lc_ref,
                 *, n_chunks, off_a, off_b):
    t = pl.program_id(1)
    lane_a = lax.broadcasted_iota(jnp.int32, (CHUNK, V7X_LANES), 1)
    low_half = lane_a < HEAD_DIM
    lane_b = lax.broadcasted_iota(jnp.int32, (CHUNK, B_W), 1) // HEAD_DIM

    def stack_heads(q, sel_ref, n_heads):
        return jnp.concatenate([q * sel_ref[h] for h in range(n_heads)], axis=0)

    def unstack_heads(r, n_heads):
        out = r[(n_heads - 1) * CHUNK:]
        for h in reversed(range(n_heads - 1)):
            out = jnp.where(lane_b == h, r[h * CHUNK:(h + 1) * CHUNK], out)
        return out

    def rows_of(j):
        return pl.ds(pl.multiple_of(j * CHUNK, CHUNK), CHUNK)

    def band(j, off, prev_chunks):
        c = t * n_chunks + j + off
        return pl.multiple_of(jnp.maximum(c - prev_chunks, 0) * CHUNK, CHUNK), jnp.minimum(c, prev_chunks)

    def scores(j, slot, first_chunks):
        rows = rows_of(j)
        start_a, var_a = band(j, off_a, A_PREV_CHUNKS)
        qa = qa_ref[0, rows, :]
        lhs = jnp.concatenate(
            [qa[:, (p // 2) * V7X_LANES:(p // 2 + 1) * V7X_LANES] * sela_ref[p % 2] for p in range(A_Q_HEADS)],
            axis=0)
        s = _dot_nt(lhs, ka_ref[0, pl.ds(start_a, A_KEYS), :])
        if first_chunks:
            s = s + maska_ref[var_a][0:1]
        sa_ref[slot, :, :A_KEYS] = s
        start_b, var_b = band(j, off_b, B_BAND_CHUNKS - 1)
        lhs = stack_heads(qb_ref[0, rows, :], selb_ref, B_HEADS)
        sb_ref[slot] = _dot_nt(lhs, kb_ref[0, pl.ds(start_b, B_KEYS), :]) + bias_ref[var_b]
        lhs = stack_heads(qc_ref[0, rows, :], selb_ref, C_HEADS)
        sc_ref[slot] = _dot_nt(lhs, mk_ref[0])

    def numerators(slot):
        for s_ref, p_ref, l_ref in ((sa_ref, pa_ref, la_ref), (sb_ref, pb_ref, lb_ref), (sc_ref, pc_ref, lc_ref)):
            s = s_ref[slot]
            e = jnp.exp2(s - jnp.max(s, axis=-1, keepdims=True))
            p_ref[slot] = e.astype(BF16)
            l_ref[slot] = jnp.broadcast_to(1.0 / jnp.sum(e, axis=-1, keepdims=True), l_ref.shape[1:])

    def outputs(j, slot):
        rows = rows_of(j)
        start_a, _ = band(j, off_a, A_PREV_CHUNKS)
        r = _dot(pa_ref[slot, :, :A_KEYS], va_ref[0, pl.ds(start_a, A_KEYS), :]) * la_ref[slot]
        for jc in range(A_Q // V7X_LANES):
            ev = r[(2 * jc) * CHUNK:(2 * jc + 1) * CHUNK]
            od = r[(2 * jc + 1) * CHUNK:(2 * jc + 2) * CHUNK]
            y_ref[0, rows, jc * V7X_LANES:(jc + 1) * V7X_LANES] = jnp.where(low_half, ev, od).astype(BF16)
        start_b, _ = band(j, off_b, B_BAND_CHUNKS - 1)
        inv_l = lb_ref[slot]
        r = _dot(pb_ref[slot], vb_ref[0, pl.ds(start_b, B_KEYS), :]) * jnp.concatenate([inv_l, inv_l], axis=1)
        y_ref[0, rows, A_Q:A_Q + B_W] = unstack_heads(r, B_HEADS).astype(BF16)
        inv_l = lc_ref[slot]
        r = _dot(pc_ref[slot], mv_ref[0]) * jnp.concatenate([inv_l, inv_l], axis=1)
        y_ref[0, rows, A_Q + B_W:] = unstack_heads(r, C_HEADS).astype(BF16)

    for slot in range(ATTN_SLOTS):
        sa_ref[slot, :, A_KEYS:] = sinkpad_ref[...]

    if n_chunks < ATTN_SLOTS:
        for j in range(n_chunks):
            scores(j, 0, j + off_a < A_PREV_CHUNKS)
            numerators(0)
            outputs(j, 0)
        return

    assert A_PREV_CHUNKS <= 2

    def step(j, slot):
        scores(j, slot, False)
        numerators((slot - 1) % ATTN_SLOTS)
        outputs(j - 2, (slot - 2) % ATTN_SLOTS)

    scores(0, 0, True)
    scores(1, 1, True)
    numerators(0)
    first = 2
    n_groups = (n_chunks - first) // ATTN_SLOTS

    def group(g, carry):
        for k in range(ATTN_SLOTS):
            step(first + g * ATTN_SLOTS + k, (first + k) % ATTN_SLOTS)
        return carry

    lax.fori_loop(0, n_groups, group, 0)
    for j in range(first + n_groups * ATTN_SLOTS, n_chunks):
        step(j, j % ATTN_SLOTS)
    numerators((n_chunks - 1) % ATTN_SLOTS)
    outputs(n_chunks - 2, (n_chunks - 2) % ATTN_SLOTS)
    outputs(n_chunks - 1, (n_chunks - 1) % ATTN_SLOTS)


def _attn_cached_kernel(qa_ref, qb_ref, qc_ref, ka_new, va_new, kb_new, vb_new, cak_ref, cav_ref, cbk_ref, cbv_ref,
                        cmk_ref, cmv_ref, bias_ref, maska_ref, sinkpad_ref, sela_ref, selb_ref, y_ref, *scratch,
                        off_a, off_b):
    *stage_scratch, ka_s, va_s, kb_s, vb_s, mk_s, mv_s = scratch
    for band_ref, cache_ref, new_ref in ((ka_s, cak_ref, ka_new), (va_s, cav_ref, va_new),
                                         (kb_s, cbk_ref, kb_new), (vb_s, cbv_ref, vb_new)):
        n_hist, n_new = cache_ref.shape[2], new_ref.shape[1]
        lead = band_ref.shape[1] - n_hist - n_new
        if lead:
            band_ref[0, :lead, :] = jnp.zeros((lead, band_ref.shape[2]), BF16)
        band_ref[0, lead:lead + n_hist, :] = cache_ref[0].T.astype(BF16)
        band_ref[0, lead + n_hist:, :] = new_ref[0]
    mk_s[0] = cmk_ref[0].T.astype(BF16)
    mv_s[0] = cmv_ref[0].T.astype(BF16)
    _attn_kernel(qa_ref, qb_ref, qc_ref, ka_s, va_s, kb_s, vb_s, mk_s, mv_s, bias_ref, maska_ref, sinkpad_ref,
                 sela_ref, selb_ref, y_ref, *stage_scratch, n_chunks=1, off_a=off_a, off_b=off_b)


def _attn(q_arrays, kv_arrays, mem_kv, bias, mask_a, sink, sel_a, sel_b, *, sq, n_chunks, off_a, off_b, caches=None):
    b = q_arrays[0][0].shape[0]
    tq = n_chunks * CHUNK
    in_specs, args = [], []
    for (arr, cb), w in zip(q_arrays, (A_Q, B_W, C_W)):
        in_specs.append(pl.BlockSpec((1, tq, w), lambda i, t, cb=cb: (i, t, cb)))
        args.append(arr)
    for arr in tuple(kv_arrays) + tuple(mem_kv if caches is None else caches):
        in_specs.append(pl.BlockSpec((1,) + arr.shape[1:], lambda i, t: (i, 0, 0)))
        args.append(arr)
    for arr in (bias, mask_a, sink, sel_a, sel_b):
        in_specs.append(_const_spec(arr.shape))
        args.append(arr)
    scratch = [pltpu.VMEM((ATTN_SLOTS, rows, cols), dtype)
               for dtype, cols_of in ((F32, (A_SCORE_COLS, B_KEYS, N_MEM)), (BF16, (A_SCORE_COLS, B_KEYS, N_MEM)),
                                      (F32, (V7X_LANES,) * 3))
               for rows, cols in zip((A_ROWS, B_ROWS, C_ROWS), cols_of)]
    if caches is None:
        body = functools.partial(_attn_kernel, n_chunks=n_chunks, off_a=off_a, off_b=off_b)
    else:
        assert n_chunks == 1
        body = functools.partial(_attn_cached_kernel, off_a=off_a, off_b=off_b)
        scratch += [pltpu.VMEM((1, rows, cols), BF16)
                    for rows, cols in ((A_KEYS, A_KV), (A_KEYS, A_KV), (B_KEYS, B_W), (B_KEYS, B_W), (N_MEM, C_W), (N_MEM, C_W))]
    return pl.pallas_call(
        body,
        grid=(b, sq // tq),
        in_specs=in_specs,
        out_specs=pl.BlockSpec((1, tq, D_MODEL), lambda i, t: (i, t, 0)),
        out_shape=jax.ShapeDtypeStruct((b, sq, D_MODEL), BF16),
        scratch_shapes=scratch,
        compiler_params=_params("parallel", "arbitrary"),
        name="attn",
    )(*args)


def _merge_kernel(h_ref, y_ref, x_ref, wgate_ref, bgate_ref, wa_ref, wb_ref, wc_ref, wout_ref, o_ref):
    sub = TOKEN_TILE // MERGE_SUBTILES
    for t in range(MERGE_SUBTILES):
        rows = slice(t * sub, (t + 1) * sub)
        h = h_ref[rows, :]
        merged = None
        col = 0
        for i, (w_ref, width) in enumerate(((wa_ref, A_Q), (wb_ref, B_W), (wc_ref, C_W))):
            gate = _sigmoid(_dot(h, wgate_ref[:, i * D_MODEL:(i + 1) * D_MODEL])
                            + bgate_ref[:, i * D_MODEL:(i + 1) * D_MODEL])
            term = gate * _dot(y_ref[rows, col:col + width], w_ref[...])
            merged = term if merged is None else merged + term
            col += width
        o_ref[rows, :] = x_ref[rows, :] + _dot(merged.astype(BF16), wout_ref[...])


def _merge(h, y, x, w_gate, b_gate, wa, wb, wc, w_out):
    n = h.shape[0]
    assert n % TOKEN_TILE == 0, n
    tile = pl.BlockSpec((TOKEN_TILE, D_MODEL), lambda i: (i, 0))
    consts = (w_gate, b_gate, wa, wb, wc, w_out)
    return pl.pallas_call(
        _merge_kernel,
        grid=(n // TOKEN_TILE,),
        in_specs=[tile, tile, tile] + [_const_spec(c.shape) for c in consts],
        out_specs=tile,
        out_shape=jax.ShapeDtypeStruct((n, D_MODEL), F32),
        compiler_params=_params("parallel"),
        name="merge",
    )(h, y, x, *consts)


def _rope_tables(pos):
    half = HEAD_DIM // 2
    inv = ROPE_THETA ** (-jnp.arange(half, dtype=F32) / half)
    ang = pos.astype(F32)[:, None] * inv[None, :]
    cos, sin = jnp.cos(ang), jnp.sin(ang)
    reps = V7X_LANES // HEAD_DIM
    return jnp.tile(jnp.concatenate([cos, cos], -1), (1, reps)), jnp.tile(jnp.concatenate([-sin, sin], -1), (1, reps))


def _bias_tables(rel_bias):
    rel_bias = rel_bias.astype(F32)
    near_lags = (REL_CLIP + CHUNK - 1) // CHUNK + 1
    ext = jnp.pad(rel_bias, ((0, 0), (0, near_lags * CHUNK - REL_CLIP)), mode="edge")
    period = 2 * CHUNK
    table = jnp.broadcast_to(rel_bias[:, None, -1:], (B_HEADS, CHUNK, B_KEYS))
    lag = jnp.arange(B_BAND_CHUNKS)[:, None, None, None] - (jnp.arange(B_KEYS) // CHUNK)[None, None, None, :]
    table = jnp.broadcast_to(table[None], (B_BAND_CHUNKS,) + table.shape)
    for g in range(near_lags):
        seg = ext[:, g * CHUNK + REL_CLIP - CHUNK + 1:g * CHUNK + REL_CLIP + CHUNK]
        ring = jnp.concatenate([seg[:, CHUNK - 1::-1], jnp.zeros((B_HEADS, 1), F32), seg[:, :CHUNK - 1:-1]], axis=1)
        block = jnp.tile(ring, (1, CHUNK))[:, :CHUNK * (period - 1)].reshape(B_HEADS, CHUNK, period - 1)[:, :, :CHUNK]
        table = jnp.where(lag == g, jnp.tile(block, (1, 1, B_BAND_CHUNKS))[None], table)
    visible = (lag >= 0) & (lag <= B_PREV_CHUNKS)
    return jnp.where(visible, table, NEG).reshape(B_BAND_CHUNKS, B_HEADS * CHUNK, B_KEYS)


def _mask_a_tables():
    j = jnp.arange(A_KEYS)[None, :]
    rows = [jnp.where((j // CHUNK) <= v, 0.0, NEG) for v in range(A_PREV_CHUNKS + 1)]
    return jnp.stack([jnp.broadcast_to(r, (8, A_KEYS)) for r in rows]).astype(F32)


def _block_diag_ones():
    i = jnp.arange(V7X_MXU_DIM)
    return (i[:, None] // HEAD_DIM == i[None, :] // HEAD_DIM).astype(BF16)


def _lane_selectors():
    lane = jnp.arange(V7X_LANES)[None, :] // HEAD_DIM
    sel_a = jnp.stack([jnp.broadcast_to(lane == p, (CHUNK, V7X_LANES)) for p in range(2)]).astype(BF16)
    lane = jnp.arange(B_W)[None, :] // HEAD_DIM
    sel_b = jnp.stack([jnp.broadcast_to(lane == h, (CHUNK, B_W)) for h in range(B_HEADS)]).astype(BF16)
    return sel_a, sel_b


def kernel(x_prompt, x_sample, cache_a_k, cache_a_v, cache_b_k, cache_b_v, cache_mem_k, cache_mem_v,
           mem_prompt, g_ff1, w_ff1_gate, w_ff1_up, w_ff1_down, g_mix, w_in, g_qa, g_ka, sinks_a,
           g_qb, g_kb, rel_bias_b, g_qc, g_mem, w_mem_kv, g_kc, w_gate, b_gate, w_br_a, w_br_b,
           w_br_c, w_out, g_ff2, w_ff2_gate, w_ff2_up, w_ff2_down, g_final):
    bp, sp, _ = x_prompt.shape
    bs, ss, _ = x_sample.shape
    l = 0
    row = lambda g: g[l].reshape(1, -1).astype(F32)
    bf = lambda w: w[l].astype(BF16)

    order = jnp.array(A_HEAD_ORDER)
    head_cols = (order[:, None] * HEAD_DIM + jnp.arange(HEAD_DIM)[None, :]).reshape(-1)
    w_in_l = w_in[l]
    w_in_bf = jnp.concatenate([w_in_l[:, :A_Q][:, head_cols], w_in_l[:, A_Q:]], axis=1).astype(BF16)
    w_br_a_bf = w_br_a[l][head_cols].astype(BF16)
    scale = HEAD_DIM ** -0.5 * LOG2E
    ones = lambda n: jnp.ones((n,), F32)
    gain = jnp.concatenate([
        jnp.tile(g_qa[l], A_Q_HEADS) * scale, jnp.tile(g_ka[l], A_KV_HEADS), ones(A_KV),
        jnp.tile(g_qb[l], B_HEADS) * scale, jnp.tile(g_kb[l], B_HEADS), ones(B_W),
        jnp.tile(g_qc[l], C_HEADS) * scale]).reshape(1, IN_COLS).astype(F32)
    sink = jnp.concatenate([jnp.repeat(sinks_a[l][order].astype(F32) * LOG2E, CHUNK)[:, None],
                            jnp.full((A_ROWS, A_SCORE_COLS - A_KEYS - 1), NEG, F32)], axis=1)
    bias = _bias_tables(rel_bias_b[l] * LOG2E)
    mask_a = _mask_a_tables()
    bd = _block_diag_ones()
    sel_a, sel_b = _lane_selectors()
    ff1 = (row(g_ff1), bf(w_ff1_gate), bf(w_ff1_up), bf(w_ff1_down), row(g_mix))
    late_f32 = (w_ff2_gate[l], w_ff2_up[l], w_ff2_down[l], w_gate[l], w_out[l])

    def trunk(x, pos, batch_view, attend, keep, late_bf16):
        n = x.shape[0] * x.shape[1]
        xf = x.reshape(n, D_MODEL)
        ffn = _ffn_stream if n == TOKEN_TILE else _ffn
        if late_bf16 is None:
            x1, h, *late_bf16 = _ffn(xf, *ff1, emit_x=True, norm_dtype=BF16, cast=late_f32)
        else:
            x1, h = ffn(xf, *ff1, emit_x=True, norm_dtype=BF16)
        w2_gate, w2_up, w2_down, w_gate_bf, w_out_bf = late_bf16
        cos, sin = _rope_tables(pos)
        q, ka16, va16, kb16, vb16, *kv32 = _proj(h.reshape(batch_view + (D_MODEL,)), w_in_bf, gain, cos, sin, bd, keep)
        y = attend(q, ka16, va16, kb16, vb16)
        x2 = _merge(h, y.reshape(n, D_MODEL), x1, w_gate_bf, row(b_gate), w_br_a_bf, bf(w_br_b), bf(w_br_c), w_out_bf)
        (out,) = ffn(x2, row(g_ff2), w2_gate, w2_up, w2_down, row(g_final), emit_x=False, norm_dtype=F32)
        return out.reshape(x.shape), kv32, late_bf16

    def split_kv(kv32, b, rows):
        kv32 = kv32.reshape(b, rows, KV_COLS)
        ka = kv32[..., :A_KV].reshape(b, rows, A_KV_HEADS, HEAD_DIM)
        va = kv32[..., A_KV:2 * A_KV].reshape(b, rows, A_KV_HEADS, HEAD_DIM)
        kb = kv32[..., 2 * A_KV:2 * A_KV + B_W].reshape(b, rows, B_HEADS, HEAD_DIM)
        vb = kv32[..., 2 * A_KV + B_W:].reshape(b, rows, B_HEADS, HEAD_DIM)
        return ka, va, kb, vb

    g_kc_row = jnp.tile(g_kc[l], C_HEADS).reshape(1, C_W).astype(F32)
    mk32t, mv32t, mk16, mv16 = _memkv(mem_prompt.reshape(bp * N_MEM, D_MODEL), row(g_mem), bf(w_mem_kv), g_kc_row, bd)
    mem_kv = (mk16.reshape(bp, N_MEM, C_W), mv16.reshape(bp, N_MEM, C_W))
    heads_last = lambda a, n_heads: a.reshape(a.shape[0], n_heads, HEAD_DIM, a.shape[-1]).transpose(0, 3, 1, 2)[None]
    mk_p = heads_last(mk32t, C_HEADS)
    mv_p = heads_last(mv32t, C_HEADS)

    q_blocks = (0, A_Q // B_W, (A_Q + B_W) // C_W)

    def attend_prompt(q, ka16, va16, kb16, vb16):
        q_arrays = tuple((q, cb) for cb in q_blocks)
        return _attn(q_arrays, (ka16, va16, kb16, vb16), mem_kv, bias, mask_a, sink, sel_a, sel_b,
                     sq=sp, n_chunks=min(ATTN_CHUNKS_PER_STEP, sp // CHUNK), off_a=0, off_b=0)

    keep = (min(A_PREV_CHUNKS * CHUNK, sp), min(B_PREV_CHUNKS * CHUNK, sp))
    y_p, tails, late_bf16 = trunk(x_prompt, jnp.arange(sp, dtype=jnp.int32), (bp, sp), attend_prompt, keep, None)
    prompt_caches = tuple(heads_last(a, n_heads) for a, n_heads in zip(tails, (A_KV_HEADS, A_KV_HEADS, B_HEADS, B_HEADS)))

    n_s = bs * ss
    rows_per_tile = TOKEN_TILE // ss
    pos_s = PAST_LEN + jnp.tile(jnp.arange(ss, dtype=jnp.int32), rows_per_tile)
    dims_first = lambda c: c[l].transpose(0, 2, 3, 1).reshape(c.shape[1], -1, c.shape[2])
    caches = tuple(dims_first(c) for c in (cache_a_k, cache_a_v, cache_b_k, cache_b_v, cache_mem_k, cache_mem_v))
    assert caches[0].shape[2] == A_KEYS - ss and caches[2].shape[2] <= B_KEYS - ss

    def attend_sample(q, ka16, va16, kb16, vb16):
        per_seq = lambda a: a.reshape(bs, ss, a.shape[-1])
        q_arrays = tuple((per_seq(q), cb) for cb in q_blocks)
        return _attn(q_arrays, tuple(per_seq(a) for a in (ka16, va16, kb16, vb16)), None, bias, mask_a, sink, sel_a, sel_b,
                     sq=ss, n_chunks=1, off_a=(A_KEYS - ss) // CHUNK, off_b=(B_KEYS - ss) // CHUNK, caches=caches)

    y_s, (kv32_s,), _ = trunk(x_sample, pos_s, (n_s // TOKEN_TILE, TOKEN_TILE), attend_sample, None, late_bf16)
    ka_s, va_s, kb_s, vb_s = split_kv(kv32_s, bs, ss)

    return (y_p, y_s, *prompt_caches, mk_p, mv_p, ka_s[None], va_s[None], kb_s[None], vb_s[None])
```
